```python
import jax, jax.numpy as jnp
from jax import lax
import numpy as np


D_MODEL = 1024
BATCH = 4
SEQ = 8192
DEPTH = 1

CHUNK = 64
GM_BLOCK = 128
GM_GROUP_DIM = 128
GM_WIDTH = D_MODEL
GM_GROUPS = GM_WIDTH // GM_GROUP_DIM
HG_DK = 128
HG_DV = 128
HG_HEADS = D_MODEL // 128
HG_KWIDTH = HG_HEADS * HG_DK
HG_WIDTH = HG_HEADS * HG_DV
HG_CHUNK = 16
D_FF = -((-8 * D_MODEL) // (3 * 256)) * 256
PLE_DIM = 256
ALPHA = (2.0 * DEPTH) ** 0.25
BETA = (8.0 * DEPTH) ** -0.25
LN_EPS = 1e-5
RMS_EPS = 1e-6
IN_SPLITS = (GM_WIDTH, GM_WIDTH, HG_KWIDTH, HG_KWIDTH, HG_WIDTH, HG_WIDTH, D_MODEL, D_MODEL)
D_IN = sum(IN_SPLITS)
IN_OFFSETS = tuple(int(o) for o in np.cumsum(IN_SPLITS)[:-1])

kernel_name = "hybrid_gmlp_hgrn2_deepnorm_block"


def _layer_norm(x, g, b):
    xf = x.astype(jnp.float32)
    xc = xf - jnp.mean(xf, -1, keepdims=True)
    var = jnp.mean(xc * xc, -1, keepdims=True)
    y = xc * lax.rsqrt(var + LN_EPS) * g.astype(jnp.float32) + b.astype(jnp.float32)
    return y.astype(x.dtype)


def _rms_norm(x, g):
    return x * lax.rsqrt(jnp.mean(x * x, -1, keepdims=True) + RMS_EPS) * g


def _gmlp_spatial_gate(u, v, v_g, v_b, w_s, b_s):
    bsz, seq, _ = v.shape
    pos = jnp.arange(GM_BLOCK)
    mask = (pos[:, None] // CHUNK) >= (pos[None, :] // CHUNK)
    w = jnp.where(mask[None], w_s, jnp.zeros_like(w_s))
    vn = _layer_norm(v, v_g, v_b).reshape(bsz, seq // GM_BLOCK, GM_BLOCK, GM_GROUPS, GM_GROUP_DIM)
    mixed = jnp.einsum("gts,bnsgc->bntgc", w, vn) + b_s.T[:, :, None]
    return u * mixed.reshape(bsz, seq, GM_WIDTH)


def _hgrn2(q_raw, f_raw, i_raw, g_raw, lb, o_g):
    bsz, seq, _ = q_raw.shape
    nc = seq // HG_CHUNK
    f32 = jnp.float32

    def heads(t, d):
        return t.astype(f32).reshape(bsz, nc, HG_CHUNK, HG_HEADS, d)

    lbf = lb.astype(f32)
    fr = f_raw.astype(f32)
    log_f = heads(jnp.log(lbf + (1.0 - lbf) * jax.nn.sigmoid(fr)), HG_DK)
    k = heads((1.0 - lbf) * jax.nn.sigmoid(-fr), HG_DK)
    q = heads(jax.nn.silu(q_raw.astype(f32)), HG_DK)
    v = heads(i_raw, HG_DV)
    b = jnp.cumsum(log_f, axis=2)
    q_dec = q * jnp.exp(b)
    k_dec = k * jnp.exp(-b)
    causal = jnp.tril(jnp.ones((HG_CHUNK, HG_CHUNK), f32))
    scores = jnp.einsum("bnthd,bnshd->bnhts", q_dec, k_dec) * causal
    o_intra = jnp.einsum("bnhts,bnshe->bnthe", scores, v)
    b_last = b[:, :, -1:]
    k_end = k * jnp.exp(b_last - b)
    decay = jnp.exp(b_last[:, :, 0])

    def step(state, xs):
        qc, kc, vc, dc = xs
        o = jnp.einsum("bthd,bhde->bthe", qc, state)
        state = dc[..., None] * state + jnp.einsum("bthd,bthe->bhde", kc, vc)
        return state, o

    s0 = jnp.zeros((bsz, HG_HEADS, HG_DK, HG_DV), f32)
    _, o_inter = lax.scan(step, s0, (jnp.moveaxis(q_dec, 1, 0), jnp.moveaxis(k_end, 1, 0),
                                     jnp.moveaxis(v, 1, 0), jnp.moveaxis(decay, 1, 0)))
    o = (o_intra + jnp.moveaxis(o_inter, 0, 1)).reshape(bsz, seq, HG_HEADS, HG_DV)
    o = _rms_norm(o, o_g.astype(f32).reshape(HG_HEADS, HG_DV)).reshape(bsz, seq, HG_WIDTH)
    o = o * jax.nn.silu(g_raw.astype(f32))
    return o.astype(q_raw.dtype)


def setup_inputs(seed: int = 0) -> dict:
    key = jax.random.key(seed)
    ks = jax.random.split(key, 26)
    f32 = jnp.float32
    L = DEPTH

    def nrm(k, shape, scale):
        return jax.random.normal(k, shape, f32) * scale

    def gain(k, shape):
        return 1.0 + nrm(k, shape, 0.02)

    return {
        "x": nrm(ks[0], (BATCH, SEQ, D_MODEL), 1.0),
        "p": nrm(ks[1], (DEPTH, BATCH, SEQ, PLE_DIM), 1.0),
        "ln0_g": gain(ks[2], (D_MODEL,)),
        "ln0_b": nrm(ks[3], (D_MODEL,), 0.02),
        "w_in": nrm(ks[4], (L, D_MODEL, D_IN), D_MODEL ** -0.5),
        "b_in": nrm(ks[5], (L, D_IN), 0.02),
        "gm_norm_g": gain(ks[6], (L, GM_WIDTH)),
        "gm_norm_b": nrm(ks[7], (L, GM_WIDTH), 0.02),
        "gm_w_s": nrm(ks[8], (L, GM_GROUPS, GM_BLOCK, GM_BLOCK), 0.5 * GM_BLOCK ** -0.5),
        "gm_b_s": gain(ks[9], (L, GM_GROUPS, GM_BLOCK)),
        "hg_lb_logits": nrm(ks[10], (L + 1, HG_KWIDTH), 0.5),
        "hg_norm_g": gain(ks[11], (L, HG_WIDTH)),
        "w_a": nrm(ks[12], (L, GM_WIDTH, D_MODEL), BETA * GM_WIDTH ** -0.5),
        "w_b": nrm(ks[13], (L, HG_WIDTH, D_MODEL), BETA * HG_WIDTH ** -0.5),
        "w_o": nrm(ks[14], (L, D_MODEL, D_MODEL), BETA * D_MODEL ** -0.5),
        "ln1_g": gain(ks[15], (L, D_MODEL)),
        "ln1_b": nrm(ks[16], (L, D_MODEL), 0.02),
        "w_ffn_gate": nrm(ks[17], (L, D_MODEL, D_FF), D_MODEL ** -0.5),
        "w_ffn_up": nrm(ks[18], (L, D_MODEL, D_FF), BETA * D_MODEL ** -0.5),
        "w_ffn_down": nrm(ks[19], (L, D_FF, D_MODEL), BETA * D_FF ** -0.5),
        "w_ple": nrm(ks[20], (L, PLE_DIM, D_MODEL), PLE_DIM ** -0.5),
        "w_ple_gate": nrm(ks[21], (L, D_MODEL, D_MODEL), D_MODEL ** -0.5),
        "b_ple_gate": nrm(ks[22], (L, D_MODEL), 0.02),
        "ln2_g": gain(ks[23], (L, D_MODEL)),
        "ln2_b": nrm(ks[24], (L, D_MODEL), 0.02),
    }


def reference(x, p, ln0_g, ln0_b, w_in, b_in, gm_norm_g, gm_norm_b, gm_w_s, gm_b_s,
              hg_lb_logits, hg_norm_g, w_a, w_b, w_o, ln1_g, ln1_b,
              w_ffn_gate, w_ffn_up, w_ffn_down, w_ple, w_ple_gate, b_ple_gate, ln2_g, ln2_b):
    lb_all = jnp.cumsum(jax.nn.softmax(hg_lb_logits.astype(jnp.float32), axis=0), axis=0)
    h = _layer_norm(x, ln0_g, ln0_b)
    for i in range(DEPTH):
        proj = jnp.einsum("bsd,de->bse", h, w_in[i]) + b_in[i]
        u_r, v_r, q_r, f_r, i_r, g_r, ga_r, gb_r = jnp.split(proj, IN_OFFSETS, axis=-1)
        y_a = _gmlp_spatial_gate(jax.nn.gelu(u_r), jax.nn.gelu(v_r), gm_norm_g[i], gm_norm_b[i],
                                 gm_w_s[i], gm_b_s[i])
        y_b = _hgrn2(q_r, f_r, i_r, g_r, lb_all[i], hg_norm_g[i])
        merged = (jax.nn.sigmoid(ga_r) * jnp.einsum("bsc,cd->bsd", y_a, w_a[i])
                  + jax.nn.sigmoid(gb_r) * jnp.einsum("bsc,cd->bsd", y_b, w_b[i]))
        mix = jnp.einsum("bsd,de->bse", merged, w_o[i])
        h1 = _layer_norm(ALPHA * h + mix, ln1_g[i], ln1_b[i])
        hid = jax.nn.silu(jnp.einsum("bsd,df->bsf", h1, w_ffn_gate[i])) * jnp.einsum("bsd,df->bsf", h1, w_ffn_up[i])
        ffn = jnp.einsum("bsf,fd->bsd", hid, w_ffn_down[i])
        ple = (jax.nn.sigmoid(jnp.einsum("bsd,de->bse", h1, w_ple_gate[i]) + b_ple_gate[i])
               * jnp.einsum("bsr,rd->bsd", p[i], w_ple[i]))
        h = _layer_norm(ALPHA * h1 + ffn + ple, ln2_g[i], ln2_b[i])
    return h
```

```python
import functools

import jax
import jax.numpy as jnp
from jax import lax
from jax.experimental import pallas as pl
from jax.experimental.pallas import tpu as pltpu

F32 = jnp.float32
BF16 = jnp.bfloat16

LN_EPS = 1e-5
RMS_EPS = 1e-6
STREAM_CHUNK = 64
GM_BLOCK = 128
GROUP = 128
HG_CHUNK = 128
HG_DIAG = 16
V7X_VMEM_LIMIT_BYTES = 56 * 1024 * 1024

TM_MIX = 256
TM_FFN = 256


def _layer_norm(x, g, b):
    mu = jnp.mean(x, -1, keepdims=True)
    xc = x - mu
    var = jnp.mean(xc * xc, -1, keepdims=True)
    return xc * lax.rsqrt(var + LN_EPS) * g + b


def _dot(a, b):
    return jnp.dot(a, b, preferred_element_type=F32)


def _dot_nt(a, b):
    return lax.dot_general(a, b, (((1,), (1,)), ((), ())), preferred_element_type=F32)


def _dot_tn(a, b):
    return lax.dot_general(a, b, (((0,), (0,)), ((), ())), preferred_element_type=F32)


def _split3(x):
    hi = x.astype(BF16)
    r1 = x - hi.astype(F32)
    mid = r1.astype(BF16)
    lo = (r1 - mid.astype(F32)).astype(BF16)
    return hi, mid, lo


def _row_ref(b, period, row):
    parts = []
    for j in range(HG_CHUNK // period):
        r = j * period + row
        parts.append(jnp.broadcast_to(b[r:r + 1, :], (period, b.shape[1])))
    return parts[0] if len(parts) == 1 else jnp.concatenate(parts, axis=0)


def _mix_kernel(x_ref, ln0g_ref, ln0b_ref, win_ref, bin_ref, gng_ref, gnb_ref, gw_ref, gbias_ref,
                lbl_ref, hng_ref, wa_ref, wb_ref, wo_ref, ln1g_ref, ln1b_ref,
                o_ref, state_ref, ya_ref, yb_ref, *, tm, width, layer, alpha):
    n_heads = width // GROUP
    n_chunks = tm // HG_CHUNK

    @pl.when(pl.program_id(1) == 0)
    def _():
        state_ref[...] = jnp.zeros_like(state_ref)

    h = _layer_norm(x_ref[...], ln0g_ref[...], ln0b_ref[...])
    hb = h.astype(BF16)

    def proj(sec):
        cols = slice(sec * width, (sec + 1) * width)
        return _dot(hb, win_ref[:, cols]) + bin_ref[:, cols]

    u = jax.nn.gelu(proj(0))
    v = jax.nn.gelu(proj(1))
    vnb = _layer_norm(v, gng_ref[...], gnb_ref[...]).astype(BF16)
    t_i = lax.broadcasted_iota(jnp.int32, (GM_BLOCK, GM_BLOCK), 0)
    s_i = lax.broadcasted_iota(jnp.int32, (GM_BLOCK, GM_BLOCK), 1)
    chunk_causal = (t_i // STREAM_CHUNK) >= (s_i // STREAM_CHUNK)
    for g in range(width // GROUP):
        cols = slice(g * GROUP, (g + 1) * GROUP)
        wg = jnp.where(chunk_causal, gw_ref[g], 0.0).astype(BF16)
        for blk in range(tm // GM_BLOCK):
            rows = slice(blk * GM_BLOCK, (blk + 1) * GM_BLOCK)
            mixed = _dot(wg, vnb[rows, cols]) + gbias_ref[:, cols]
            ya_ref[rows, cols] = (u[rows, cols] * mixed).astype(BF16)

    logits = lbl_ref[...]
    e_l = jnp.exp(logits - jnp.max(logits, axis=0, keepdims=True))
    lb = jnp.sum(e_l[:layer + 1], axis=0, keepdims=True) / jnp.sum(e_l, axis=0, keepdims=True)
    q_all = jax.nn.silu(proj(2))
    f_raw = proj(3)
    logf_all = jnp.log(lb + (1.0 - lb) * jax.nn.sigmoid(f_raw))
    k_all = (1.0 - lb) * jax.nn.sigmoid(-f_raw)
    v_all = proj(4)
    gate_all = jax.nn.silu(proj(5))
    hng = hng_ref[...]

    tril = (t_i >= s_i)
    cum_mat = jnp.where(tril, 1.0, 0.0).astype(BF16)
    level_masks = []
    m = HG_CHUNK // 2
    while m >= HG_DIAG:
        level_masks.append((m, (t_i // (2 * m) == s_i // (2 * m)) & (t_i % (2 * m) >= m)
                            & (s_i % (2 * m) < m)))
        m //= 2
    diag_mask = (t_i // HG_DIAG == s_i // HG_DIAG) & tril

    for c in range(n_chunks):
        rows = slice(c * HG_CHUNK, (c + 1) * HG_CHUNK)
        q, k, vv, logf = q_all[rows], k_all[rows], v_all[rows], logf_all[rows]
        hi, mid, lo = _split3(logf)
        b = _dot(cum_mat, hi) + _dot(cum_mat, mid) + _dot(cum_mat, lo)
        b_last = b[HG_CHUNK - 1:HG_CHUNK, :]
        q_dec = (q * jnp.exp(b)).astype(BF16)
        k_end = (k * jnp.exp(b_last - b)).astype(BF16)
        decay = jnp.exp(b_last)
        vb = vv.astype(BF16)
        lvl = []
        for m, mask in level_masks:
            e = jnp.exp(-jnp.abs(b - _row_ref(b, 2 * m, m - 1)))
            lvl.append(((q * e).astype(BF16), (k * e).astype(BF16), mask))
        dq = b - (_row_ref(b, HG_DIAG, 0) - _row_ref(logf, HG_DIAG, 0))
        q_d = (q * jnp.exp(dq)).astype(BF16)
        k_d = (k * jnp.exp(-dq)).astype(BF16)
        for hd in range(n_heads):
            cols = slice(hd * GROUP, (hd + 1) * GROUP)
            scores = jnp.where(diag_mask, _dot_nt(q_d[:, cols], k_d[:, cols]), 0.0)
            for ql, kl, mask in lvl:
                scores = jnp.where(mask, _dot_nt(ql[:, cols], kl[:, cols]), scores)
            st = state_ref[hd]
            o = _dot(scores.astype(BF16), vb[:, cols]) + _dot_nt(q_dec[:, cols], st.astype(BF16))
            state_ref[hd] = st * decay[:, cols] + _dot_tn(vb[:, cols], k_end[:, cols])
            o = o * lax.rsqrt(jnp.mean(o * o, -1, keepdims=True) + RMS_EPS) * hng[:, cols]
            yb_ref[rows, cols] = (o * gate_all[rows, cols]).astype(BF16)

    merged = (jax.nn.sigmoid(proj(6)) * _dot(ya_ref[...], wa_ref[...])
              + jax.nn.sigmoid(proj(7)) * _dot(yb_ref[...], wb_ref[...]))
    mix = _dot(merged.astype(BF16), wo_ref[...])
    o_ref[...] = _layer_norm(alpha * h + mix, ln1g_ref[...], ln1b_ref[...])


def _ffn_kernel(h_ref, p_ref, wg_ref, wu_ref, wd_ref, wpg_ref, bpg_ref, wple_ref, ln2g_ref, ln2b_ref,
                o_ref, *, alpha):
    h1 = h_ref[...]
    hb = h1.astype(BF16)
    hid = jax.nn.silu(_dot(hb, wg_ref[...])) * _dot(hb, wu_ref[...])
    ffn = _dot(hid.astype(BF16), wd_ref[...])
    ple = (jax.nn.sigmoid(_dot(hb, wpg_ref[...]) + bpg_ref[...])
           * _dot(p_ref[...].astype(BF16), wple_ref[...]))
    o_ref[...] = _layer_norm(alpha * h1 + ffn + ple, ln2g_ref[...], ln2b_ref[...])


def _resident(arr):
    nd = arr.ndim
    return pl.BlockSpec(arr.shape, lambda *_: (0,) * nd, pipeline_mode=pl.Buffered(1))


def kernel(x, p, ln0_g, ln0_b, w_in, b_in, gm_norm_g, gm_norm_b, gm_w_s, gm_b_s, hg_lb_logits, hg_norm_g, w_a, w_b, w_o, ln1_g, ln1_b, w_ffn_gate, w_ffn_up, w_ffn_down, w_ple, w_ple_gate, b_ple_gate, ln2_g, ln2_b):
    bsz, seq, d = x.shape
    depth = w_in.shape[0]
    assert depth == 1, "single-layer block"
    assert seq % TM_MIX == 0 and (bsz * seq) % TM_FFN == 0 and TM_MIX % HG_CHUNK == 0
    assert w_in.shape[2] == 8 * d and d % GROUP == 0
    alpha = (2.0 * depth) ** 0.25
    layer = 0
    row = lambda a: a.reshape(1, -1).astype(F32)

    gbias = jnp.repeat(gm_b_s[layer].T, GROUP, axis=1)
    mix_consts = [row(ln0_g), row(ln0_b), w_in[layer].astype(BF16), row(b_in[layer]),
                  row(gm_norm_g[layer]), row(gm_norm_b[layer]), gm_w_s[layer], gbias,
                  hg_lb_logits.astype(F32), row(hg_norm_g[layer]),
                  w_a[layer].astype(BF16), w_b[layer].astype(BF16), w_o[layer].astype(BF16),
                  row(ln1_g[layer]), row(ln1_b[layer])]
    h1 = pl.pallas_call(
        functools.partial(_mix_kernel, tm=TM_MIX, width=d, layer=layer, alpha=alpha),
        name="token_mixing",
        grid=(bsz, seq // TM_MIX),
        in_specs=[pl.BlockSpec((None, TM_MIX, d), lambda b, s: (b, s, 0))]
                 + [_resident(a) for a in mix_consts],
        out_specs=pl.BlockSpec((None, TM_MIX, d), lambda b, s: (b, s, 0)),
        out_shape=jax.ShapeDtypeStruct((bsz, seq, d), F32),
        scratch_shapes=[pltpu.VMEM((d // GROUP, GROUP, GROUP), F32),
                        pltpu.VMEM((TM_MIX, d), BF16),
                        pltpu.VMEM((TM_MIX, d), BF16)],
        compiler_params=pltpu.CompilerParams(
            dimension_semantics=("arbitrary", "arbitrary"),
            vmem_limit_bytes=V7X_VMEM_LIMIT_BYTES),
    )(x, *mix_consts)

    n_tok = bsz * seq
    ple_dim = p.shape[-1]
    ffn_consts = [w_ffn_gate[layer].astype(BF16), w_ffn_up[layer].astype(BF16),
                  w_ffn_down[layer].astype(BF16), w_ple_gate[layer].astype(BF16),
                  row(b_ple_gate[layer]), w_ple[layer].astype(BF16),
                  row(ln2_g[layer]), row(ln2_b[layer])]
    out = pl.pallas_call(
        functools.partial(_ffn_kernel, alpha=alpha),
        name="channel_mixing",
        grid=(n_tok // TM_FFN,),
        in_specs=[pl.BlockSpec((TM_FFN, d), lambda i: (i, 0)),
                  pl.BlockSpec((TM_FFN, ple_dim), lambda i: (i, 0))]
                 + [_resident(a) for a in ffn_consts],
        out_specs=pl.BlockSpec((TM_FFN, d), lambda i: (i, 0)),
        out_shape=jax.ShapeDtypeStruct((n_tok, d), F32),
        compiler_params=pltpu.CompilerParams(
            dimension_semantics=("arbitrary",),
            vmem_limit_bytes=V7X_VMEM_LIMIT_BYTES),
    )(h1.reshape(n_tok, d), p[layer].reshape(n_tok, ple_dim), *ffn_consts)
    return out.reshape(bsz, seq, d)
```

```python
import functools

import jax
import jax.numpy as jnp
from jax import lax
from jax.experimental import pallas as pl
from jax.experimental.pallas import tpu as pltpu

F32 = jnp.float32
BF16 = jnp.bfloat16

LN_EPS = 1e-5
RMS_EPS = 1e-6
STREAM_CHUNK = 64
GM_BLOCK = 128
GROUP = 128
HG_CHUNK = 128
HG_DIAG = 16
V7X_VMEM_LIMIT_BYTES = 56 * 1024 * 1024

TM_MIX = 256
TM_FFN = 256


def _layer_norm(x, g, b):
    mu = jnp.mean(x, -1, keepdims=True)
    xc = x - mu
    var = jnp.mean(xc * xc, -1, keepdims=True)
    return xc * lax.rsqrt(var + LN_EPS) * g + b


def _dot(a, b):
    return jnp.dot(a, b, preferred_element_type=F32)


def _dot_nt(a, b):
    return lax.dot_general(a, b, (((1,), (1,)), ((), ())), preferred_element_type=F32)


def _dot_tn(a, b):
    return lax.dot_general(a, b, (((0,), (0,)), ((), ())), preferred_element_type=F32)


def _split3(x):
    hi = x.astype(BF16)
    r1 = x - hi.astype(F32)
    mid = r1.astype(BF16)
    lo = (r1 - mid.astype(F32)).astype(BF16)
    return hi, mid, lo


def _row_ref(b, period, row):
    parts = []
    for j in range(HG_CHUNK // period):
        r = j * period + row
        parts.append(jnp.broadcast_to(b[r:r + 1, :], (period, b.shape[1])))
    return parts[0] if len(parts) == 1 else jnp.concatenate(parts, axis=0)


def _mix_kernel(x_ref, ln0g_ref, ln0b_ref, win_ref, bin_ref, gng_ref, gnb_ref, gw_ref, gbias_ref,
                lbl_ref, hng_ref, wa_ref, wb_ref, wo_ref, ln1g_ref, ln1b_ref,
                o_ref, state_ref, ya_ref, yb_ref, *, tm, width, layer, alpha):
    n_heads = width // GROUP
    n_chunks = tm // HG_CHUNK

    @pl.when(pl.program_id(1) == 0)
    def _():
        state_ref[...] = jnp.zeros_like(state_ref)

    h = _layer_norm(x_ref[...], ln0g_ref[...], ln0b_ref[...])
    hb = h.astype(BF16)

    def proj(sec):
        cols = slice(sec * width, (sec + 1) * width)
        return _dot(hb, win_ref[:, cols]) + bin_ref[:, cols]

    u = jax.nn.gelu(proj(0))
    v = jax.nn.gelu(proj(1))
    vnb = _layer_norm(v, gng_ref[...], gnb_ref[...]).astype(BF16)
    t_i = lax.broadcasted_iota(jnp.int32, (GM_BLOCK, GM_BLOCK), 0)
    s_i = lax.broadcasted_iota(jnp.int32, (GM_BLOCK, GM_BLOCK), 1)
    chunk_causal = (t_i // STREAM_CHUNK) >= (s_i // STREAM_CHUNK)
    for g in range(width // GROUP):
        cols = slice(g * GROUP, (g + 1) * GROUP)
        wg = jnp.where(chunk_causal, gw_ref[g], 0.0).astype(BF16)
        for blk in range(0, tm // GM_BLOCK, 2):
            r0 = slice(blk * GM_BLOCK, (blk + 1) * GM_BLOCK)
            r1 = slice((blk + 1) * GM_BLOCK, (blk + 2) * GM_BLOCK)
            mixed = _dot(wg, jnp.concatenate([vnb[r0, cols], vnb[r1, cols]], axis=1))
            ya_ref[r0, cols] = (u[r0, cols] * (mixed[:, :GROUP] + gbias_ref[:, cols])).astype(BF16)
            ya_ref[r1, cols] = (u[r1, cols] * (mixed[:, GROUP:] + gbias_ref[:, cols])).astype(BF16)

    logits = lbl_ref[...]
    e_l = jnp.exp(logits - jnp.max(logits, axis=0, keepdims=True))
    lb = jnp.sum(e_l[:layer + 1], axis=0, keepdims=True) / jnp.sum(e_l, axis=0, keepdims=True)
    q_all = jax.nn.silu(proj(2))
    sig_f = jax.nn.sigmoid(proj(3))
    logf_all = jnp.log(lb + (1.0 - lb) * sig_f)
    k_all = (1.0 - lb) * (1.0 - sig_f)
    v_all = proj(4)
    gate_all = jax.nn.silu(proj(5))
    hng = hng_ref[...]

    cum_mat = jnp.where(t_i >= s_i, 1.0, 0.0).astype(BF16)
    cum_mat3 = jnp.concatenate([cum_mat, cum_mat, cum_mat], axis=1)
    pair = 2 * GROUP
    t_p = lax.broadcasted_iota(jnp.int32, (HG_CHUNK, pair), 0)
    s_p = lax.broadcasted_iota(jnp.int32, (HG_CHUNK, pair), 1) % GROUP
    level_masks = []
    m = HG_CHUNK // 2
    while m >= HG_DIAG:
        level_masks.append((m, (t_p // (2 * m) == s_p // (2 * m)) & (t_p % (2 * m) >= m)
                            & (s_p % (2 * m) < m)))
        m //= 2
    diag_mask = (t_p // HG_DIAG == s_p // HG_DIAG) & (t_p >= s_p)
    same_head = (lax.broadcasted_iota(jnp.int32, (pair, pair), 0) // GROUP
                 == lax.broadcasted_iota(jnp.int32, (pair, pair), 1) // GROUP)
    zero_blk = jnp.zeros((HG_CHUNK, GROUP), BF16)

    def block_diag(xp):
        return jnp.concatenate([jnp.concatenate([xp[:, :GROUP], zero_blk], axis=1),
                                jnp.concatenate([zero_blk, xp[:, GROUP:]], axis=1)], axis=0)

    for c in range(n_chunks):
        rows = slice(c * HG_CHUNK, (c + 1) * HG_CHUNK)
        q, k, vv, logf = q_all[rows], k_all[rows], v_all[rows], logf_all[rows]
        b = _dot(cum_mat3, jnp.concatenate(_split3(logf), axis=0))
        b_last = b[HG_CHUNK - 1:HG_CHUNK, :]
        q_dec = (q * jnp.exp(b)).astype(BF16)
        k_end = (k * jnp.exp(b_last - b)).astype(BF16)
        decay = jnp.exp(b_last)
        vb = vv.astype(BF16)
        lvl = []
        for m, mask in level_masks:
            e = jnp.exp(-jnp.abs(b - _row_ref(b, 2 * m, m - 1)))
            lvl.append(((q * e).astype(BF16), (k * e).astype(BF16), mask))
        dq = b - (_row_ref(b, HG_DIAG, 0) - _row_ref(logf, HG_DIAG, 0))
        q_d = (q * jnp.exp(dq)).astype(BF16)
        k_d = (k * jnp.exp(-dq)).astype(BF16)
        for pr in range(n_heads // 2):
            pc = slice(pr * pair, (pr + 1) * pair)
            scores = jnp.where(diag_mask, _dot_nt(q_d[:, pc], block_diag(k_d[:, pc])), 0.0)
            for ql, kl, mask in lvl:
                scores = jnp.where(mask, _dot_nt(ql[:, pc], block_diag(kl[:, pc])), scores)
            st = state_ref[pr]
            o = (_dot(scores.astype(BF16), block_diag(vb[:, pc]))
                 + _dot_nt(q_dec[:, pc], st.astype(BF16)))
            upd = _dot_tn(vb[:, pc], k_end[:, pc])
            state_ref[pr] = st * decay[:, pc] + jnp.where(same_head, upd, 0.0)
            for hh in range(2):
                cols = slice(pr * pair + hh * GROUP, pr * pair + (hh + 1) * GROUP)
                oh = o[:, hh * GROUP:(hh + 1) * GROUP]
                oh = oh * lax.rsqrt(jnp.mean(oh * oh, -1, keepdims=True) + RMS_EPS) * hng[:, cols]
                yb_ref[rows, cols] = (oh * gate_all[rows, cols]).astype(BF16)

    merged = (jax.nn.sigmoid(proj(6)) * _dot(ya_ref[...], wa_ref[...])
              + jax.nn.sigmoid(proj(7)) * _dot(yb_ref[...], wb_ref[...]))
    mix = _dot(merged.astype(BF16), wo_ref[...])
    o_ref[...] = _layer_norm(alpha * h + mix, ln1g_ref[...], ln1b_ref[...])


def _ffn_kernel(h_ref, p_ref, wg_ref, wu_ref, wd_ref, wpg_ref, bpg_ref, wple_ref, ln2g_ref, ln2b_ref,
                o_ref, *, alpha):
    h1 = h_ref[...]
    hb = h1.astype(BF16)
    hid = jax.nn.silu(_dot(hb, wg_ref[...])) * _dot(hb, wu_ref[...])
    ffn = _dot(hid.astype(BF16), wd_ref[...])
    ple = (jax.nn.sigmoid(_dot(hb, wpg_ref[...]) + bpg_ref[...])
           * _dot(p_ref[...].astype(BF16), wple_ref[...]))
    o_ref[...] = _layer_norm(alpha * h1 + ffn + ple, ln2g_ref[...], ln2b_ref[...])


def _resident(arr):
    nd = arr.ndim
    return pl.BlockSpec(arr.shape, lambda *_: (0,) * nd, pipeline_mode=pl.Buffered(1))


def kernel(x, p, ln0_g, ln0_b, w_in, b_in, gm_norm_g, gm_norm_b, gm_w_s, gm_b_s, hg_lb_logits, hg_norm_g, w_a, w_b, w_o, ln1_g, ln1_b, w_ffn_gate, w_ffn_up, w_ffn_down, w_ple, w_ple_gate, b_ple_gate, ln2_g, ln2_b):
    bsz, seq, d = x.shape
    depth = w_in.shape[0]
    assert depth == 1, "single-layer block"
    assert seq % TM_MIX == 0 and (bsz * seq) % TM_FFN == 0 and TM_MIX % (2 * HG_CHUNK) == 0
    assert w_in.shape[2] == 8 * d and d % (2 * GROUP) == 0
    alpha = (2.0 * depth) ** 0.25
    layer = 0
    row = lambda a: a.reshape(1, -1).astype(F32)

    gbias = jnp.repeat(gm_b_s[layer].T, GROUP, axis=1)
    mix_consts = [row(ln0_g), row(ln0_b), w_in[layer].astype(BF16), row(b_in[layer]),
                  row(gm_norm_g[layer]), row(gm_norm_b[layer]), gm_w_s[layer], gbias,
                  hg_lb_logits.astype(F32), row(hg_norm_g[layer]),
                  w_a[layer].astype(BF16), w_b[layer].astype(BF16), w_o[layer].astype(BF16),
                  row(ln1_g[layer]), row(ln1_b[layer])]
    h1 = pl.pallas_call(
        functools.partial(_mix_kernel, tm=TM_MIX, width=d, layer=layer, alpha=alpha),
        name="token_mixing",
        grid=(bsz, seq // TM_MIX),
        in_specs=[pl.BlockSpec((None, TM_MIX, d), lambda b, s: (b, s, 0))]
                 + [_resident(a) for a in mix_consts],
        out_specs=pl.BlockSpec((None, TM_MIX, d), lambda b, s: (b, s, 0)),
        out_shape=jax.ShapeDtypeStruct((bsz, seq, d), F32),
        scratch_shapes=[pltpu.VMEM((d // (2 * GROUP), 2 * GROUP, 2 * GROUP), F32),
                        pltpu.VMEM((TM_MIX, d), BF16),
                        pltpu.VMEM((TM_MIX, d), BF16)],
        compiler_params=pltpu.CompilerParams(
            dimension_semantics=("arbitrary", "arbitrary"),
            vmem_limit_bytes=V7X_VMEM_LIMIT_BYTES),
    )(x, *mix_consts)

    n_tok = bsz * seq
    ple_dim = p.shape[-1]
    ffn_consts = [w_ffn_gate[layer].astype(BF16), w_ffn_up[layer].astype(BF16),
                  w_ffn_down[layer].astype(BF16), w_ple_gate[layer].astype(BF16),
                  row(b_ple_gate[layer]), w_ple[layer].astype(BF16),
                  row(ln2_g[layer]), row(ln2_b[layer])]
    out = pl.pallas_call(
        functools.partial(_ffn_kernel, alpha=alpha),
        name="channel_mixing",
        grid=(n_tok // TM_FFN,),
        in_specs=[pl.BlockSpec((TM_FFN, d), lambda i: (i, 0)),
                  pl.BlockSpec((TM_FFN, ple_dim), lambda i: (i, 0))]
                 + [_resident(a) for a in ffn_consts],
        out_specs=pl.BlockSpec((TM_FFN, d), lambda i: (i, 0)),
        out_shape=jax.ShapeDtypeStruct((n_tok, d), F32),
        compiler_params=pltpu.CompilerParams(
            dimension_semantics=("arbitrary",),
            vmem_limit_bytes=V7X_VMEM_LIMIT_BYTES),
    )(h1.reshape(n_tok, d), p[layer].reshape(n_tok, ple_dim), *ffn_consts)
    return out.reshape(bsz, seq, d)
```

```python
import functools

import jax
import jax.numpy as jnp
from jax import lax
from jax.experimental import pallas as pl
from jax.experimental.pallas import tpu as pltpu

F32 = jnp.float32
BF16 = jnp.bfloat16

LN_EPS = 1e-5
RMS_EPS = 1e-6
STREAM_CHUNK = 64
GM_BLOCK = 128
GROUP = 128
HG_CHUNK = 128
HG_DIAG = 16
V7X_VMEM_LIMIT_BYTES = 56 * 1024 * 1024

TM_MIX = 256
TM_FFN = 256


def _layer_norm(x, g, b):
    mu = jnp.mean(x, -1, keepdims=True)
    xc = x - mu
    var = jnp.mean(xc * xc, -1, keepdims=True)
    return xc * lax.rsqrt(var + LN_EPS) * g + b


def _dot(a, b):
    return jnp.dot(a, b, preferred_element_type=F32)


def _dot_nt(a, b):
    return lax.dot_general(a, b, (((1,), (1,)), ((), ())), preferred_element_type=F32)


def _dot_tn(a, b):
    return lax.dot_general(a, b, (((0,), (0,)), ((), ())), preferred_element_type=F32)


def _pack_rows(w):
    k, n = w.shape
    bits = lax.bitcast_convert_type(w.astype(BF16), jnp.uint16).astype(jnp.uint32).reshape(k // 2, 2, n)
    return bits[:, 0, :] | (bits[:, 1, :] << 16)


def _bf16_rows(packed):
    return pltpu.bitcast(packed, BF16)


def _split3(x):
    hi = x.astype(BF16)
    r1 = x - hi.astype(F32)
    mid = r1.astype(BF16)
    lo = (r1 - mid.astype(F32)).astype(BF16)
    return hi, mid, lo


def _row_ref(b, period, row):
    parts = []
    for j in range(HG_CHUNK // period):
        r = j * period + row
        parts.append(jnp.broadcast_to(b[r:r + 1, :], (period, b.shape[1])))
    return parts[0] if len(parts) == 1 else jnp.concatenate(parts, axis=0)


def _mix_kernel(x_ref, ln0g_ref, ln0b_ref, win_ref, bin_ref, gng_ref, gnb_ref, gw_ref, gbias_ref,
                lbl_ref, hng_ref, wa_ref, wb_ref, wo_ref, ln1g_ref, ln1b_ref,
                o_ref, state_ref, ya_ref, yb_ref, *, tm, width, layer, alpha):
    n_heads = width // GROUP
    n_chunks = tm // HG_CHUNK

    @pl.when(pl.program_id(1) == 0)
    def _():
        state_ref[...] = jnp.zeros_like(state_ref)

    h = _layer_norm(x_ref[...], ln0g_ref[...], ln0b_ref[...])
    hb = h.astype(BF16)

    def proj(sec):
        cols = slice(sec * width, (sec + 1) * width)
        return _dot(hb, _bf16_rows(win_ref[:, cols])) + bin_ref[:, cols]

    t_i = lax.broadcasted_iota(jnp.int32, (GM_BLOCK, GM_BLOCK), 0)
    s_i = lax.broadcasted_iota(jnp.int32, (GM_BLOCK, GM_BLOCK), 1)

    logits = lbl_ref[...]
    e_l = jnp.exp(logits - jnp.max(logits, axis=0, keepdims=True))
    lb = jnp.sum(e_l[:layer + 1], axis=0, keepdims=True) / jnp.sum(e_l, axis=0, keepdims=True)
    q_all = jax.nn.silu(proj(2))
    sig_f = jax.nn.sigmoid(proj(3))
    logf_all = jnp.log(lb + (1.0 - lb) * sig_f)
    k_all = (1.0 - lb) * (1.0 - sig_f)
    v_all = proj(4)
    gate_all = jax.nn.silu(proj(5))
    hng = hng_ref[...]

    env = {}

    def gmlp_u():
        env["u"] = jax.nn.gelu(proj(0))

    def gmlp_v():
        v = jax.nn.gelu(proj(1))
        env["vnb"] = _layer_norm(v, gng_ref[...], gnb_ref[...]).astype(BF16)

    def gmlp_mix(groups):
        chunk_causal = (t_i // STREAM_CHUNK) >= (s_i // STREAM_CHUNK)
        u, vnb = env["u"], env["vnb"]
        for g in groups:
            cols = slice(g * GROUP, (g + 1) * GROUP)
            wg = jnp.where(chunk_causal, gw_ref[g], 0.0).astype(BF16)
            for blk in range(0, tm // GM_BLOCK, 2):
                r0 = slice(blk * GM_BLOCK, (blk + 1) * GM_BLOCK)
                r1 = slice((blk + 1) * GM_BLOCK, (blk + 2) * GM_BLOCK)
                mixed = _dot(wg, jnp.concatenate([vnb[r0, cols], vnb[r1, cols]], axis=1))
                ya_ref[r0, cols] = (u[r0, cols] * (mixed[:, :GROUP] + gbias_ref[:, cols])).astype(BF16)
                ya_ref[r1, cols] = (u[r1, cols] * (mixed[:, GROUP:] + gbias_ref[:, cols])).astype(BF16)

    def gate_a():
        env["ga"] = jax.nn.sigmoid(proj(6))

    def branch_a():
        env["ma"] = env["ga"] * _dot(ya_ref[...], _bf16_rows(wa_ref[...]))

    def gate_b():
        env["gb"] = jax.nn.sigmoid(proj(7))

    n_groups = width // GROUP
    fillers = [gmlp_u, gmlp_v,
               functools.partial(gmlp_mix, range(0, n_groups // 2)),
               functools.partial(gmlp_mix, range(n_groups // 2, n_groups)),
               gate_a, branch_a, gate_b]

    cum_mat = jnp.where(t_i >= s_i, 1.0, 0.0).astype(BF16)
    cum_mat3 = jnp.concatenate([cum_mat, cum_mat, cum_mat], axis=1)
    pair = 2 * GROUP
    t_p = lax.broadcasted_iota(jnp.int32, (HG_CHUNK, pair), 0)
    s_p = lax.broadcasted_iota(jnp.int32, (HG_CHUNK, pair), 1) % GROUP
    level_masks = []
    m = HG_CHUNK // 2
    while m >= HG_DIAG:
        level_masks.append((m, (t_p // (2 * m) == s_p // (2 * m)) & (t_p % (2 * m) >= m)
                            & (s_p % (2 * m) < m)))
        m //= 2
    diag_mask = (t_p // HG_DIAG == s_p // HG_DIAG) & (t_p >= s_p)
    same_head = (lax.broadcasted_iota(jnp.int32, (pair, pair), 0) // GROUP
                 == lax.broadcasted_iota(jnp.int32, (pair, pair), 1) // GROUP)
    zero_blk = jnp.zeros((HG_CHUNK, GROUP), BF16)

    def block_diag(xp):
        return jnp.concatenate([jnp.concatenate([xp[:, :GROUP], zero_blk], axis=1),
                                jnp.concatenate([zero_blk, xp[:, GROUP:]], axis=1)], axis=0)

    for c in range(n_chunks):
        rows = slice(c * HG_CHUNK, (c + 1) * HG_CHUNK)
        q, k, vv, logf = q_all[rows], k_all[rows], v_all[rows], logf_all[rows]
        b = _dot(cum_mat3, jnp.concatenate(_split3(logf), axis=0))
        b_last = b[HG_CHUNK - 1:HG_CHUNK, :]
        q_dec = (q * jnp.exp(b)).astype(BF16)
        k_end = (k * jnp.exp(b_last - b)).astype(BF16)
        decay = jnp.exp(b_last)
        vb = vv.astype(BF16)
        lvl = []
        for m, mask in level_masks:
            e = jnp.exp(-jnp.abs(b - _row_ref(b, 2 * m, m - 1)))
            lvl.append(((q * e).astype(BF16), (k * e).astype(BF16), mask))
        dq = b - (_row_ref(b, HG_DIAG, 0) - _row_ref(logf, HG_DIAG, 0))
        q_d = (q * jnp.exp(dq)).astype(BF16)
        k_d = (k * jnp.exp(-dq)).astype(BF16)
        for pr in range(n_heads // 2):
            pc = slice(pr * pair, (pr + 1) * pair)
            scores = jnp.where(diag_mask, _dot_nt(q_d[:, pc], block_diag(k_d[:, pc])), 0.0)
            for ql, kl, mask in lvl:
                scores = jnp.where(mask, _dot_nt(ql[:, pc], block_diag(kl[:, pc])), scores)
            st = state_ref[pr]
            o = (_dot(scores.astype(BF16), block_diag(vb[:, pc]))
                 + _dot_nt(q_dec[:, pc], st.astype(BF16)))
            upd = _dot_tn(vb[:, pc], k_end[:, pc])
            state_ref[pr] = st * decay[:, pc] + jnp.where(same_head, upd, 0.0)
            for hh in range(2):
                cols = slice(pr * pair + hh * GROUP, pr * pair + (hh + 1) * GROUP)
                oh = o[:, hh * GROUP:(hh + 1) * GROUP]
                oh = oh * lax.rsqrt(jnp.mean(oh * oh, -1, keepdims=True) + RMS_EPS) * hng[:, cols]
                yb_ref[rows, cols] = (oh * gate_all[rows, cols]).astype(BF16)
            if fillers:
                fillers.pop(0)()
    while fillers:
        fillers.pop(0)()

    merged = env["ma"] + env["gb"] * _dot(yb_ref[...], _bf16_rows(wb_ref[...]))
    mix = _dot(merged.astype(BF16), _bf16_rows(wo_ref[...]))
    o_ref[...] = _layer_norm(alpha * h + mix, ln1g_ref[...], ln1b_ref[...])


def _ffn_kernel(h_ref, p_ref, wg_ref, wu_ref, wd_ref, wpg_ref, bpg_ref, wple_ref, ln2g_ref, ln2b_ref,
                o_ref, *, alpha):
    h1 = h_ref[...]
    hb = h1.astype(BF16)
    hid = jax.nn.silu(_dot(hb, _bf16_rows(wg_ref[...]))) * _dot(hb, _bf16_rows(wu_ref[...]))
    ffn = _dot(hid.astype(BF16), _bf16_rows(wd_ref[...]))
    ple = (jax.nn.sigmoid(_dot(hb, _bf16_rows(wpg_ref[...])) + bpg_ref[...])
           * _dot(p_ref[...].astype(BF16), _bf16_rows(wple_ref[...])))
    o_ref[...] = _layer_norm(alpha * h1 + ffn + ple, ln2g_ref[...], ln2b_ref[...])


def _resident(arr):
    nd = arr.ndim
    return pl.BlockSpec(arr.shape, lambda *_: (0,) * nd, pipeline_mode=pl.Buffered(1))


def kernel(x, p, ln0_g, ln0_b, w_in, b_in, gm_norm_g, gm_norm_b, gm_w_s, gm_b_s, hg_lb_logits, hg_norm_g, w_a, w_b, w_o, ln1_g, ln1_b, w_ffn_gate, w_ffn_up, w_ffn_down, w_ple, w_ple_gate, b_ple_gate, ln2_g, ln2_b):
    bsz, seq, d = x.shape
    depth = w_in.shape[0]
    assert depth == 1, "single-layer block"
    assert seq % TM_MIX == 0 and (bsz * seq) % TM_FFN == 0 and TM_MIX % (2 * HG_CHUNK) == 0
    assert w_in.shape[2] == 8 * d and d % (2 * GROUP) == 0
    alpha = (2.0 * depth) ** 0.25
    layer = 0
    row = lambda a: a.reshape(1, -1).astype(F32)

    gbias = jnp.repeat(gm_b_s[layer].T, GROUP, axis=1)
    mix_consts = [row(ln0_g), row(ln0_b), _pack_rows(w_in[layer]), row(b_in[layer]),
                  row(gm_norm_g[layer]), row(gm_norm_b[layer]), gm_w_s[layer], gbias,
                  hg_lb_logits.astype(F32), row(hg_norm_g[layer]),
                  _pack_rows(w_a[layer]), _pack_rows(w_b[layer]), _pack_rows(w_o[layer]),
                  row(ln1_g[layer]), row(ln1_b[layer])]
    h1 = pl.pallas_call(
        functools.partial(_mix_kernel, tm=TM_MIX, width=d, layer=layer, alpha=alpha),
        name="token_mixing",
        grid=(bsz, seq // TM_MIX),
        in_specs=[pl.BlockSpec((None, TM_MIX, d), lambda b, s: (b, s, 0))]
                 + [_resident(a) for a in mix_consts],
        out_specs=pl.BlockSpec((None, TM_MIX, d), lambda b, s: (b, s, 0)),
        out_shape=jax.ShapeDtypeStruct((bsz, seq, d), F32),
        scratch_shapes=[pltpu.VMEM((d // (2 * GROUP), 2 * GROUP, 2 * GROUP), F32),
                        pltpu.VMEM((TM_MIX, d), BF16),
                        pltpu.VMEM((TM_MIX, d), BF16)],
        compiler_params=pltpu.CompilerParams(
            dimension_semantics=("arbitrary", "arbitrary"),
            vmem_limit_bytes=V7X_VMEM_LIMIT_BYTES),
    )(x, *mix_consts)

    n_tok = bsz * seq
    ple_dim = p.shape[-1]
    ffn_consts = [_pack_rows(w_ffn_gate[layer]), _pack_rows(w_ffn_up[layer]),
                  _pack_rows(w_ffn_down[layer]), _pack_rows(w_ple_gate[layer]),
                  row(b_ple_gate[layer]), _pack_rows(w_ple[layer]),
                  row(ln2_g[layer]), row(ln2_b[layer])]
    out = pl.pallas_call(
        functools.partial(_ffn_kernel, alpha=alpha),
        name="channel_mixing",
        grid=(n_tok // TM_FFN,),
        in_specs=[pl.BlockSpec((TM_FFN, d), lambda i: (i, 0)),
                  pl.BlockSpec((TM_FFN, ple_dim), lambda i: (i, 0))]
                 + [_resident(a) for a in ffn_consts],
        out_specs=pl.BlockSpec((TM_FFN, d), lambda i: (i, 0)),
        out_shape=jax.ShapeDtypeStruct((n_tok, d), F32),
        compiler_params=pltpu.CompilerParams(
            dimension_semantics=("arbitrary",),
            vmem_limit_bytes=V7X_VMEM_LIMIT_BYTES),
    )(h1.reshape(n_tok, d), p[layer].reshape(n_tok, ple_dim), *ffn_consts)
    return out.reshape(bsz, seq, d)
```

```python
import functools

import jax
import jax.numpy as jnp
from jax import lax
from jax.experimental import pallas as pl
from jax.experimental.pallas import tpu as pltpu

F32 = jnp.float32
BF16 = jnp.bfloat16

LN_EPS = 1e-5
RMS_EPS = 1e-6
STREAM_CHUNK = 64
GM_BLOCK = 128
GROUP = 128
HG_CHUNK = 128
HG_DIAG = 16
V7X_VMEM_LIMIT_BYTES = 56 * 1024 * 1024

TM_MIX = 256
TM_FFN = 256


def _layer_norm(x, g, b):
    mu = jnp.mean(x, -1, keepdims=True)
    xc = x - mu
    var = jnp.mean(xc * xc, -1, keepdims=True)
    return xc * lax.rsqrt(var + LN_EPS) * g + b


def _dot(a, b):
    return jnp.dot(a, b, preferred_element_type=F32)


def _dot_nt(a, b):
    return lax.dot_general(a, b, (((1,), (1,)), ((), ())), preferred_element_type=F32)


def _dot_tn(a, b):
    return lax.dot_general(a, b, (((0,), (0,)), ((), ())), preferred_element_type=F32)


def _split3(x):
    hi = x.astype(BF16)
    r1 = x - hi.astype(F32)
    mid = r1.astype(BF16)
    lo = (r1 - mid.astype(F32)).astype(BF16)
    return hi, mid, lo


def _row_ref(b, period, row):
    parts = []
    for j in range(HG_CHUNK // period):
        r = j * period + row
        parts.append(jnp.broadcast_to(b[r:r + 1, :], (period, b.shape[1])))
    return parts[0] if len(parts) == 1 else jnp.concatenate(parts, axis=0)


def _mix_kernel(x_ref, ln0g_ref, ln0b_ref, win_ref, bin_ref, gng_ref, gnb_ref, gw_ref, gbias_ref,
                lbl_ref, hng_ref, wa_ref, wb_ref, wo_ref, ln1g_ref, ln1b_ref,
                o_ref, state_ref, ya_ref, yb_ref, *, tm, width, layer, alpha):
    n_heads = width // GROUP
    n_chunks = tm // HG_CHUNK

    @pl.when(pl.program_id(1) == 0)
    def _():
        state_ref[...] = jnp.zeros_like(state_ref)

    h = _layer_norm(x_ref[...], ln0g_ref[...], ln0b_ref[...])
    hb = h.astype(BF16)

    def proj(sec):
        cols = slice(sec * width, (sec + 1) * width)
        return _dot(hb, win_ref[:, cols]) + bin_ref[:, cols]

    t_i = lax.broadcasted_iota(jnp.int32, (GM_BLOCK, GM_BLOCK), 0)
    s_i = lax.broadcasted_iota(jnp.int32, (GM_BLOCK, GM_BLOCK), 1)

    logits = lbl_ref[...]
    e_l = jnp.exp(logits - jnp.max(logits, axis=0, keepdims=True))
    lb = jnp.sum(e_l[:layer + 1], axis=0, keepdims=True) / jnp.sum(e_l, axis=0, keepdims=True)
    q_all = jax.nn.silu(proj(2))
    sig_f = jax.nn.sigmoid(proj(3))
    logf_all = jnp.log(lb + (1.0 - lb) * sig_f)
    k_all = (1.0 - lb) * (1.0 - sig_f)
    v_all = proj(4)
    gate_all = jax.nn.silu(proj(5))
    hng = hng_ref[...]

    env = {}

    def gmlp_u():
        env["u"] = jax.nn.gelu(proj(0))

    def gmlp_v():
        v = jax.nn.gelu(proj(1))
        env["vnb"] = _layer_norm(v, gng_ref[...], gnb_ref[...]).astype(BF16)

    def gmlp_mix(groups):
        chunk_causal = (t_i // STREAM_CHUNK) >= (s_i // STREAM_CHUNK)
        u, vnb = env["u"], env["vnb"]
        for g in groups:
            cols = slice(g * GROUP, (g + 1) * GROUP)
            wg = jnp.where(chunk_causal, gw_ref[g], 0.0).astype(BF16)
            for blk in range(0, tm // GM_BLOCK, 2):
                r0 = slice(blk * GM_BLOCK, (blk + 1) * GM_BLOCK)
                r1 = slice((blk + 1) * GM_BLOCK, (blk + 2) * GM_BLOCK)
                mixed = _dot(wg, jnp.concatenate([vnb[r0, cols], vnb[r1, cols]], axis=1))
                ya_ref[r0, cols] = (u[r0, cols] * (mixed[:, :GROUP] + gbias_ref[:, cols])).astype(BF16)
                ya_ref[r1, cols] = (u[r1, cols] * (mixed[:, GROUP:] + gbias_ref[:, cols])).astype(BF16)

    def gate_a():
        env["ga"] = jax.nn.sigmoid(proj(6))

    def branch_a():
        env["ma"] = env["ga"] * _dot(ya_ref[...], wa_ref[...])

    def gate_b():
        env["gb"] = jax.nn.sigmoid(proj(7))

    n_groups = width // GROUP
    fillers = [gmlp_u, gmlp_v,
               functools.partial(gmlp_mix, range(0, n_groups // 2)),
               functools.partial(gmlp_mix, range(n_groups // 2, n_groups)),
               gate_a, branch_a, gate_b]

    cum_mat = jnp.where(t_i >= s_i, 1.0, 0.0).astype(BF16)
    cum_mat3 = jnp.concatenate([cum_mat, cum_mat, cum_mat], axis=1)
    pair = 2 * GROUP
    t_p = lax.broadcasted_iota(jnp.int32, (HG_CHUNK, pair), 0)
    s_p = lax.broadcasted_iota(jnp.int32, (HG_CHUNK, pair), 1) % GROUP
    level_masks = []
    m = HG_CHUNK // 2
    while m >= HG_DIAG:
        level_masks.append((m, (t_p // (2 * m) == s_p // (2 * m)) & (t_p % (2 * m) >= m)
                            & (s_p % (2 * m) < m)))
        m //= 2
    diag_mask = (t_p // HG_DIAG == s_p // HG_DIAG) & (t_p >= s_p)
    same_head = (lax.broadcasted_iota(jnp.int32, (pair, pair), 0) // GROUP
                 == lax.broadcasted_iota(jnp.int32, (pair, pair), 1) // GROUP)
    zero_blk = jnp.zeros((HG_CHUNK, GROUP), BF16)

    def block_diag(xp):
        return jnp.concatenate([jnp.concatenate([xp[:, :GROUP], zero_blk], axis=1),
                                jnp.concatenate([zero_blk, xp[:, GROUP:]], axis=1)], axis=0)

    for c in range(n_chunks):
        rows = slice(c * HG_CHUNK, (c + 1) * HG_CHUNK)
        q, k, vv, logf = q_all[rows], k_all[rows], v_all[rows], logf_all[rows]
        b = _dot(cum_mat3, jnp.concatenate(_split3(logf), axis=0))
        b_last = b[HG_CHUNK - 1:HG_CHUNK, :]
        q_dec = (q * jnp.exp(b)).astype(BF16)
        k_end = (k * jnp.exp(b_last - b)).astype(BF16)
        decay = jnp.exp(b_last)
        vb = vv.astype(BF16)
        lvl = []
        for m, mask in level_masks:
            e = jnp.exp(-jnp.abs(b - _row_ref(b, 2 * m, m - 1)))
            lvl.append(((q * e).astype(BF16), (k * e).astype(BF16), mask))
        dq = b - (_row_ref(b, HG_DIAG, 0) - _row_ref(logf, HG_DIAG, 0))
        q_d = (q * jnp.exp(dq)).astype(BF16)
        k_d = (k * jnp.exp(-dq)).astype(BF16)
        for pr in range(n_heads // 2):
            pc = slice(pr * pair, (pr + 1) * pair)
            scores = jnp.where(diag_mask, _dot_nt(q_d[:, pc], block_diag(k_d[:, pc])), 0.0)
            for ql, kl, mask in lvl:
                scores = jnp.where(mask, _dot_nt(ql[:, pc], block_diag(kl[:, pc])), scores)
            st = state_ref[pr]
            o = (_dot(scores.astype(BF16), block_diag(vb[:, pc]))
                 + _dot_nt(q_dec[:, pc], st.astype(BF16)))
            upd = _dot_tn(vb[:, pc], k_end[:, pc])
            state_ref[pr] = st * decay[:, pc] + jnp.where(same_head, upd, 0.0)
            for hh in range(2):
                cols = slice(pr * pair + hh * GROUP, pr * pair + (hh + 1) * GROUP)
                oh = o[:, hh * GROUP:(hh + 1) * GROUP]
                oh = oh * lax.rsqrt(jnp.mean(oh * oh, -1, keepdims=True) + RMS_EPS) * hng[:, cols]
                yb_ref[rows, cols] = (oh * gate_all[rows, cols]).astype(BF16)
            if fillers:
                fillers.pop(0)()
    while fillers:
        fillers.pop(0)()

    merged = env["ma"] + env["gb"] * _dot(yb_ref[...], wb_ref[...])
    mix = _dot(merged.astype(BF16), wo_ref[...])
    o_ref[...] = _layer_norm(alpha * h + mix, ln1g_ref[...], ln1b_ref[...])


def _ffn_kernel(h_ref, p_ref, wg_ref, wu_ref, wd_ref, wpg_ref, bpg_ref, wple_ref, ln2g_ref, ln2b_ref,
                o_ref, *, alpha):
    h1 = h_ref[...]
    hb = h1.astype(BF16)
    hid = jax.nn.silu(_dot(hb, wg_ref[...])) * _dot(hb, wu_ref[...])
    ffn = _dot(hid.astype(BF16), wd_ref[...])
    ple = (jax.nn.sigmoid(_dot(hb, wpg_ref[...]) + bpg_ref[...])
           * _dot(p_ref[...].astype(BF16), wple_ref[...]))
    o_ref[...] = _layer_norm(alpha * h1 + ffn + ple, ln2g_ref[...], ln2b_ref[...])


def _resident(arr):
    nd = arr.ndim
    return pl.BlockSpec(arr.shape, lambda *_: (0,) * nd, pipeline_mode=pl.Buffered(1))


def kernel(x, p, ln0_g, ln0_b, w_in, b_in, gm_norm_g, gm_norm_b, gm_w_s, gm_b_s, hg_lb_logits, hg_norm_g, w_a, w_b, w_o, ln1_g, ln1_b, w_ffn_gate, w_ffn_up, w_ffn_down, w_ple, w_ple_gate, b_ple_gate, ln2_g, ln2_b):
    bsz, seq, d = x.shape
    depth = w_in.shape[0]
    assert depth == 1, "single-layer block"
    assert seq % TM_MIX == 0 and (bsz * seq) % TM_FFN == 0 and TM_MIX % (2 * HG_CHUNK) == 0
    assert w_in.shape[2] == 8 * d and d % (2 * GROUP) == 0
    alpha = (2.0 * depth) ** 0.25
    layer = 0
    row = lambda a: a.reshape(1, -1).astype(F32)

    gbias = jnp.repeat(gm_b_s[layer].T, GROUP, axis=1)
    mix_consts = [row(ln0_g), row(ln0_b), w_in[layer].astype(BF16), row(b_in[layer]),
                  row(gm_norm_g[layer]), row(gm_norm_b[layer]), gm_w_s[layer], gbias,
                  hg_lb_logits.astype(F32), row(hg_norm_g[layer]),
                  w_a[layer].astype(BF16), w_b[layer].astype(BF16), w_o[layer].astype(BF16),
                  row(ln1_g[layer]), row(ln1_b[layer])]
    h1 = pl.pallas_call(
        functools.partial(_mix_kernel, tm=TM_MIX, width=d, layer=layer, alpha=alpha),
        name="token_mixing",
        grid=(bsz, seq // TM_MIX),
        in_specs=[pl.BlockSpec((None, TM_MIX, d), lambda b, s: (b, s, 0))]
                 + [_resident(a) for a in mix_consts],
        out_specs=pl.BlockSpec((None, TM_MIX, d), lambda b, s: (b, s, 0)),
        out_shape=jax.ShapeDtypeStruct((bsz, seq, d), F32),
        scratch_shapes=[pltpu.VMEM((d // (2 * GROUP), 2 * GROUP, 2 * GROUP), F32),
                        pltpu.VMEM((TM_MIX, d), BF16),
                        pltpu.VMEM((TM_MIX, d), BF16)],
        compiler_params=pltpu.CompilerParams(
            dimension_semantics=("arbitrary", "arbitrary"),
            vmem_limit_bytes=V7X_VMEM_LIMIT_BYTES),
    )(x, *mix_consts)

    n_tok = bsz * seq
    ple_dim = p.shape[-1]
    ffn_consts = [w_ffn_gate[layer].astype(BF16), w_ffn_up[layer].astype(BF16),
                  w_ffn_down[layer].astype(BF16), w_ple_gate[layer].astype(BF16),
                  row(b_ple_gate[layer]), w_ple[layer].astype(BF16),
                  row(ln2_g[layer]), row(ln2_b[layer])]
    out = pl.pallas_call(
        functools.partial(_ffn_kernel, alpha=alpha),
        name="channel_mixing",
        grid=(n_tok // TM_FFN,),
        in_specs=[pl.BlockSpec((TM_FFN, d), lambda i: (i, 0)),
                  pl.BlockSpec((TM_FFN, ple_dim), lambda i: (i, 0))]
                 + [_resident(a) for a in ffn_consts],
        out_specs=pl.BlockSpec((TM_FFN, d), lambda i: (i, 0)),
        out_shape=jax.ShapeDtypeStruct((n_tok, d), F32),
        compiler_params=pltpu.CompilerParams(
            dimension_semantics=("arbitrary",),
            vmem_limit_bytes=V7X_VMEM_LIMIT_BYTES),
    )(h1.reshape(n_tok, d), p[layer].reshape(n_tok, ple_dim), *ffn_consts)
    return out.reshape(bsz, seq, d)
```

```python
import functools

import jax
import jax.numpy as jnp
from jax import lax
from jax.experimental import pallas as pl
from jax.experimental.pallas import tpu as pltpu

F32 = jnp.float32
BF16 = jnp.bfloat16

LN_EPS = 1e-5
RMS_EPS = 1e-6
STREAM_CHUNK = 64
GM_BLOCK = 128
GROUP = 128
HG_CHUNK = 128
HG_DIAG = 16
V7X_VMEM_LIMIT_BYTES = 56 * 1024 * 1024

TM_MIX = 256
TM_FFN = 512


def _layer_norm(x, g, b):
    mu = jnp.mean(x, -1, keepdims=True)
    xc = x - mu
    var = jnp.mean(xc * xc, -1, keepdims=True)
    return xc * lax.rsqrt(var + LN_EPS) * g + b


def _dot(a, b):
    return jnp.dot(a, b, preferred_element_type=F32)


def _dot_nt(a, b):
    return lax.dot_general(a, b, (((1,), (1,)), ((), ())), preferred_element_type=F32)


def _dot_tn(a, b):
    return lax.dot_general(a, b, (((0,), (0,)), ((), ())), preferred_element_type=F32)


def _split3(x):
    hi = x.astype(BF16)
    r1 = x - hi.astype(F32)
    mid = r1.astype(BF16)
    lo = (r1 - mid.astype(F32)).astype(BF16)
    return hi, mid, lo


def _row_ref(b, period, row):
    parts = []
    for j in range(HG_CHUNK // period):
        r = j * period + row
        parts.append(jnp.broadcast_to(b[r:r + 1, :], (period, b.shape[1])))
    return parts[0] if len(parts) == 1 else jnp.concatenate(parts, axis=0)


def _mix_kernel(x_ref, ln0g_ref, ln0b_ref, win_ref, bin_ref, gng_ref, gnb_ref, gw_ref, gbias_ref,
                lbl_ref, hng_ref, wa_ref, wb_ref, wo_ref, ln1g_ref, ln1b_ref,
                o_ref, state_ref, ya_ref, yb_ref, *, tm, width, layer, alpha):
    n_heads = width // GROUP
    n_chunks = tm // HG_CHUNK

    @pl.when(pl.program_id(1) == 0)
    def _():
        state_ref[...] = jnp.zeros_like(state_ref)

    h = _layer_norm(x_ref[...], ln0g_ref[...], ln0b_ref[...])
    hb = h.astype(BF16)

    def proj(sec):
        cols = slice(sec * width, (sec + 1) * width)
        return _dot(hb, win_ref[:, cols]) + bin_ref[:, cols]

    t_i = lax.broadcasted_iota(jnp.int32, (GM_BLOCK, GM_BLOCK), 0)
    s_i = lax.broadcasted_iota(jnp.int32, (GM_BLOCK, GM_BLOCK), 1)

    logits = lbl_ref[...]
    e_l = jnp.exp(logits - jnp.max(logits, axis=0, keepdims=True))
    lb = jnp.sum(e_l[:layer + 1], axis=0, keepdims=True) / jnp.sum(e_l, axis=0, keepdims=True)
    sig_f = jax.nn.sigmoid(proj(3))
    logf_all = jnp.log2(lb + (1.0 - lb) * sig_f)
    k_all = (1.0 - lb) * (1.0 - sig_f)
    q_all = jax.nn.silu(proj(2))
    hng = hng_ref[...]

    env = {}

    def gmlp_u():
        env["u"] = jax.nn.gelu(proj(0))

    def gmlp_v():
        v = jax.nn.gelu(proj(1))
        env["vnb"] = _layer_norm(v, gng_ref[...], gnb_ref[...]).astype(BF16)

    def gmlp_mix(groups):
        chunk_causal = (t_i // STREAM_CHUNK) >= (s_i // STREAM_CHUNK)
        u, vnb = env["u"], env["vnb"]
        for g in groups:
            cols = slice(g * GROUP, (g + 1) * GROUP)
            wg = jnp.where(chunk_causal, gw_ref[g], 0.0).astype(BF16)
            for blk in range(0, tm // GM_BLOCK, 2):
                r0 = slice(blk * GM_BLOCK, (blk + 1) * GM_BLOCK)
                r1 = slice((blk + 1) * GM_BLOCK, (blk + 2) * GM_BLOCK)
                mixed = _dot(wg, jnp.concatenate([vnb[r0, cols], vnb[r1, cols]], axis=1))
                ya_ref[r0, cols] = (u[r0, cols] * (mixed[:, :GROUP] + gbias_ref[:, cols])).astype(BF16)
                ya_ref[r1, cols] = (u[r1, cols] * (mixed[:, GROUP:] + gbias_ref[:, cols])).astype(BF16)

    def gate_a():
        env["ga"] = jax.nn.sigmoid(proj(6))

    def branch_a():
        env["ma"] = env["ga"] * _dot(ya_ref[...], wa_ref[...])

    def gate_b():
        env["gb"] = jax.nn.sigmoid(proj(7))

    n_groups = width // GROUP
    fillers = [gmlp_u, gmlp_v,
               functools.partial(gmlp_mix, range(0, n_groups // 2)),
               functools.partial(gmlp_mix, range(n_groups // 2, n_groups)),
               gate_a, branch_a, gate_b]

    cum_mat = jnp.where(t_i >= s_i, 1.0, 0.0).astype(BF16)
    cum_mat3 = jnp.concatenate([cum_mat, cum_mat, cum_mat], axis=1)
    pair = 2 * GROUP
    t_p = lax.broadcasted_iota(jnp.int32, (HG_CHUNK, pair), 0)
    s_p = lax.broadcasted_iota(jnp.int32, (HG_CHUNK, pair), 1) % GROUP
    level_masks = []
    m = HG_CHUNK // 2
    while m >= HG_DIAG:
        level_masks.append((m, (t_p // (2 * m) == s_p // (2 * m)) & (t_p % (2 * m) >= m)
                            & (s_p % (2 * m) < m)))
        m //= 2
    diag_mask = (t_p // HG_DIAG == s_p // HG_DIAG) & (t_p >= s_p)
    same_head = (lax.broadcasted_iota(jnp.int32, (pair, pair), 0) // GROUP
                 == lax.broadcasted_iota(jnp.int32, (pair, pair), 1) // GROUP)
    zero_blk = jnp.zeros((HG_CHUNK, GROUP), BF16)

    def block_diag(xp):
        return jnp.concatenate([jnp.concatenate([xp[:, :GROUP], zero_blk], axis=1),
                                jnp.concatenate([zero_blk, xp[:, GROUP:]], axis=1)], axis=0)

    def chunk_prep(c):
        rows = slice(c * HG_CHUNK, (c + 1) * HG_CHUNK)
        q, k, logf = q_all[rows], k_all[rows], logf_all[rows]
        b = _dot(cum_mat3, jnp.concatenate(_split3(logf), axis=0))
        b_last = b[HG_CHUNK - 1:HG_CHUNK, :]
        ch = {"rows": rows}
        ch["q_dec"] = (q * jnp.exp2(b)).astype(BF16)
        ch["k_end"] = (k * jnp.exp2(b_last - b)).astype(BF16)
        ch["decay"] = jnp.exp2(b_last)
        lvl = []
        for m, mask in level_masks:
            e = jnp.exp2(-jnp.abs(b - _row_ref(b, 2 * m, m - 1)))
            lvl.append(((q * e).astype(BF16), (k * e).astype(BF16), mask))
        ch["lvl"] = lvl
        dq = b - (_row_ref(b, HG_DIAG, 0) - _row_ref(logf, HG_DIAG, 0))
        ch["q_d"] = (q * jnp.exp2(dq)).astype(BF16)
        ch["k_d"] = (k * jnp.exp2(-dq)).astype(BF16)
        return ch

    def head_pair(ch, pr):
        rows = ch["rows"]
        vb = late["v"][rows].astype(BF16)
        pc = slice(pr * pair, (pr + 1) * pair)
        scores = jnp.where(diag_mask, _dot_nt(ch["q_d"][:, pc], block_diag(ch["k_d"][:, pc])), 0.0)
        for ql, kl, mask in ch["lvl"]:
            scores = jnp.where(mask, _dot_nt(ql[:, pc], block_diag(kl[:, pc])), scores)
        st = state_ref[pr]
        o = (_dot(scores.astype(BF16), block_diag(vb[:, pc]))
             + _dot_nt(ch["q_dec"][:, pc], st.astype(BF16)))
        upd = _dot_tn(vb[:, pc], ch["k_end"][:, pc])
        state_ref[pr] = st * ch["decay"][:, pc] + jnp.where(same_head, upd, 0.0)
        for hh in range(2):
            cols = slice(pr * pair + hh * GROUP, pr * pair + (hh + 1) * GROUP)
            oh = o[:, hh * GROUP:(hh + 1) * GROUP]
            oh = oh * lax.rsqrt(jnp.mean(oh * oh, -1, keepdims=True) + RMS_EPS) * hng[:, cols]
            yb_ref[rows, cols] = (oh * late["gate"][rows, cols]).astype(BF16)

    late = {}

    def proj_v():
        late["v"] = proj(4)

    def proj_gate():
        late["gate"] = jax.nn.silu(proj(5))

    early_fillers = [proj_v, proj_gate]
    chunks = []
    for c in range(n_chunks):
        chunks.append(chunk_prep(c))
        if early_fillers:
            early_fillers.pop(0)()
    while early_fillers:
        early_fillers.pop(0)()
    for ch in chunks:
        for pr in range(n_heads // 2):
            head_pair(ch, pr)
            if fillers:
                fillers.pop(0)()
    while fillers:
        fillers.pop(0)()

    merged = env["ma"] + env["gb"] * _dot(yb_ref[...], wb_ref[...])
    mix = _dot(merged.astype(BF16), wo_ref[...])
    o_ref[...] = _layer_norm(alpha * h + mix, ln1g_ref[...], ln1b_ref[...])


def _ffn_kernel(h_ref, p_ref, wg_ref, wu_ref, wd_ref, wpg_ref, bpg_ref, wple_ref, ln2g_ref, ln2b_ref,
                o_ref, *, alpha):
    h1 = h_ref[...]
    hb = h1.astype(BF16)
    hid = jax.nn.silu(_dot(hb, wg_ref[...])) * _dot(hb, wu_ref[...])
    ffn = _dot(hid.astype(BF16), wd_ref[...])
    ple = (jax.nn.sigmoid(_dot(hb, wpg_ref[...]) + bpg_ref[...])
           * _dot(p_ref[...].astype(BF16), wple_ref[...]))
    o_ref[...] = _layer_norm(alpha * h1 + ffn + ple, ln2g_ref[...], ln2b_ref[...])


def _resident(arr):
    nd = arr.ndim
    return pl.BlockSpec(arr.shape, lambda *_: (0,) * nd, pipeline_mode=pl.Buffered(1))


def kernel(x, p, ln0_g, ln0_b, w_in, b_in, gm_norm_g, gm_norm_b, gm_w_s, gm_b_s, hg_lb_logits, hg_norm_g, w_a, w_b, w_o, ln1_g, ln1_b, w_ffn_gate, w_ffn_up, w_ffn_down, w_ple, w_ple_gate, b_ple_gate, ln2_g, ln2_b):
    bsz, seq, d = x.shape
    depth = w_in.shape[0]
    assert depth == 1, "single-layer block"
    assert seq % TM_MIX == 0 and (bsz * seq) % TM_FFN == 0 and TM_MIX % (2 * HG_CHUNK) == 0
    assert w_in.shape[2] == 8 * d and d % (2 * GROUP) == 0
    alpha = (2.0 * depth) ** 0.25
    layer = 0
    row = lambda a: a.reshape(1, -1).astype(F32)

    gbias = jnp.repeat(gm_b_s[layer].T, GROUP, axis=1)
    mix_consts = [row(ln0_g), row(ln0_b), w_in[layer].astype(BF16), row(b_in[layer]),
                  row(gm_norm_g[layer]), row(gm_norm_b[layer]), gm_w_s[layer], gbias,
                  hg_lb_logits.astype(F32), row(hg_norm_g[layer]),
                  w_a[layer].astype(BF16), w_b[layer].astype(BF16), w_o[layer].astype(BF16),
                  row(ln1_g[layer]), row(ln1_b[layer])]
    h1 = pl.pallas_call(
        functools.partial(_mix_kernel, tm=TM_MIX, width=d, layer=layer, alpha=alpha),
        name="token_mixing",
        grid=(bsz, seq // TM_MIX),
        in_specs=[pl.BlockSpec((None, TM_MIX, d), lambda b, s: (b, s, 0))]
                 + [_resident(a) for a in mix_consts],
        out_specs=pl.BlockSpec((None, TM_MIX, d), lambda b, s: (b, s, 0)),
        out_shape=jax.ShapeDtypeStruct((bsz, seq, d), F32),
        scratch_shapes=[pltpu.VMEM((d // (2 * GROUP), 2 * GROUP, 2 * GROUP), F32),
                        pltpu.VMEM((TM_MIX, d), BF16),
                        pltpu.VMEM((TM_MIX, d), BF16)],
        compiler_params=pltpu.CompilerParams(
            dimension_semantics=("arbitrary", "arbitrary"),
            vmem_limit_bytes=V7X_VMEM_LIMIT_BYTES),
    )(x, *mix_consts)

    n_tok = bsz * seq
    ple_dim = p.shape[-1]
    ffn_consts = [w_ffn_gate[layer].astype(BF16), w_ffn_up[layer].astype(BF16),
                  w_ffn_down[layer].astype(BF16), w_ple_gate[layer].astype(BF16),
                  row(b_ple_gate[layer]), w_ple[layer].astype(BF16),
                  row(ln2_g[layer]), row(ln2_b[layer])]
    out = pl.pallas_call(
        functools.partial(_ffn_kernel, alpha=alpha),
        name="channel_mixing",
        grid=(n_tok // TM_FFN,),
        in_specs=[pl.BlockSpec((TM_FFN, d), lambda i: (i, 0)),
                  pl.BlockSpec((TM_FFN, ple_dim), lambda i: (i, 0))]
                 + [_resident(a) for a in ffn_consts],
        out_specs=pl.BlockSpec((TM_FFN, d), lambda i: (i, 0)),
        out_shape=jax.ShapeDtypeStruct((n_tok, d), F32),
        compiler_params=pltpu.CompilerParams(
            dimension_semantics=("arbitrary",),
            vmem_limit_bytes=V7X_VMEM_LIMIT_BYTES),
    )(h1.reshape(n_tok, d), p[layer].reshape(n_tok, ple_dim), *ffn_consts)
    return out.reshape(bsz, seq, d)
```

```python
import functools

import jax
import jax.numpy as jnp
from jax import lax
from jax.experimental import pallas as pl
from jax.experimental.pallas import tpu as pltpu

F32 = jnp.float32
BF16 = jnp.bfloat16

LN_EPS = 1e-5
RMS_EPS = 1e-6
STREAM_CHUNK = 64
GM_BLOCK = 128
GROUP = 128
HG_CHUNK = 128
HG_DIAG = 16
V7X_VMEM_LIMIT_BYTES = 56 * 1024 * 1024

TM_MIX = 512
SUB_MIX = 256
TM_FFN = 512
SUB_FFN = 256


def _layer_norm(x, g, b):
    mu = jnp.mean(x, -1, keepdims=True)
    xc = x - mu
    var = jnp.mean(xc * xc, -1, keepdims=True)
    return xc * lax.rsqrt(var + LN_EPS) * g + b


def _dot(a, b):
    return jnp.dot(a, b, preferred_element_type=F32)


def _dot_nt(a, b):
    return lax.dot_general(a, b, (((1,), (1,)), ((), ())), preferred_element_type=F32)


def _dot_tn(a, b):
    return lax.dot_general(a, b, (((0,), (0,)), ((), ())), preferred_element_type=F32)


def _split3(x):
    hi = x.astype(BF16)
    r1 = x - hi.astype(F32)
    mid = r1.astype(BF16)
    lo = (r1 - mid.astype(F32)).astype(BF16)
    return hi, mid, lo


def _row_ref(b, period, row):
    parts = []
    for j in range(HG_CHUNK // period):
        r = j * period + row
        parts.append(jnp.broadcast_to(b[r:r + 1, :], (period, b.shape[1])))
    return parts[0] if len(parts) == 1 else jnp.concatenate(parts, axis=0)


def _mix_kernel(x_ref, ln0g_ref, ln0b_ref, win_ref, bin_ref, gng_ref, gnb_ref, gw_ref, gbias_ref,
                lbl_ref, hng_ref, wa_ref, wb_ref, wo_ref, ln1g_ref, ln1b_ref,
                o_ref, state_ref, ya_ref, yb_ref, *, tm, sub, width, layer, alpha):
    n_heads = width // GROUP
    n_groups = width // GROUP
    n_chunks = sub // HG_CHUNK
    pair = 2 * GROUP

    @pl.when(pl.program_id(1) == 0)
    def _():
        state_ref[...] = jnp.zeros_like(state_ref)

    t_i = lax.broadcasted_iota(jnp.int32, (GM_BLOCK, GM_BLOCK), 0)
    s_i = lax.broadcasted_iota(jnp.int32, (GM_BLOCK, GM_BLOCK), 1)
    logits = lbl_ref[...]
    e_l = jnp.exp(logits - jnp.max(logits, axis=0, keepdims=True))
    lb = jnp.sum(e_l[:layer + 1], axis=0, keepdims=True) / jnp.sum(e_l, axis=0, keepdims=True)
    hng = hng_ref[...]
    cum_mat = jnp.where(t_i >= s_i, 1.0, 0.0).astype(BF16)
    cum_mat3 = jnp.concatenate([cum_mat, cum_mat, cum_mat], axis=1)
    t_p = lax.broadcasted_iota(jnp.int32, (HG_CHUNK, pair), 0)
    s_p = lax.broadcasted_iota(jnp.int32, (HG_CHUNK, pair), 1) % GROUP
    level_masks = []
    m = HG_CHUNK // 2
    while m >= HG_DIAG:
        level_masks.append((m, (t_p // (2 * m) == s_p // (2 * m)) & (t_p % (2 * m) >= m)
                            & (s_p % (2 * m) < m)))
        m //= 2
    diag_mask = (t_p // HG_DIAG == s_p // HG_DIAG) & (t_p >= s_p)
    same_head = (lax.broadcasted_iota(jnp.int32, (pair, pair), 0) // GROUP
                 == lax.broadcasted_iota(jnp.int32, (pair, pair), 1) // GROUP)
    zero_blk = jnp.zeros((HG_CHUNK, GROUP), BF16)

    def block_diag(xp):
        return jnp.concatenate([jnp.concatenate([xp[:, :GROUP], zero_blk], axis=1),
                                jnp.concatenate([zero_blk, xp[:, GROUP:]], axis=1)], axis=0)

    def sub_tile(r0):
        env = {}

        def ln0():
            h = _layer_norm(x_ref[r0:r0 + sub, :], ln0g_ref[...], ln0b_ref[...])
            env["h"] = h
            env["hb"] = h.astype(BF16)

        def proj(sec):
            cols = slice(sec * width, (sec + 1) * width)
            return _dot(env["hb"], win_ref[:, cols]) + bin_ref[:, cols]

        def hgrn_inputs():
            sig_f = jax.nn.sigmoid(proj(3))
            env["logf"] = jnp.log2(lb + (1.0 - lb) * sig_f)
            env["k"] = (1.0 - lb) * (1.0 - sig_f)
            env["q"] = jax.nn.silu(proj(2))

        def chunk_prep(c):
            rows = slice(c * HG_CHUNK, (c + 1) * HG_CHUNK)
            q, k, logf = env["q"][rows], env["k"][rows], env["logf"][rows]
            b = _dot(cum_mat3, jnp.concatenate(_split3(logf), axis=0))
            b_last = b[HG_CHUNK - 1:HG_CHUNK, :]
            ch = {"rows": rows}
            ch["q_dec"] = (q * jnp.exp2(b)).astype(BF16)
            ch["k_end"] = (k * jnp.exp2(b_last - b)).astype(BF16)
            ch["decay"] = jnp.exp2(b_last)
            lvl = []
            for m, mask in level_masks:
                e = jnp.exp2(-jnp.abs(b - _row_ref(b, 2 * m, m - 1)))
                lvl.append(((q * e).astype(BF16), (k * e).astype(BF16), mask))
            ch["lvl"] = lvl
            dq = b - (_row_ref(b, HG_DIAG, 0) - _row_ref(logf, HG_DIAG, 0))
            ch["q_d"] = (q * jnp.exp2(dq)).astype(BF16)
            ch["k_d"] = (k * jnp.exp2(-dq)).astype(BF16)
            env["chunk", c] = ch

        def proj_v():
            env["v"] = proj(4)

        def proj_gate():
            env["gate"] = jax.nn.silu(proj(5))

        def head_pair(c, pr):
            ch = env["chunk", c]
            rows = ch["rows"]
            out_rows = slice(r0 + rows.start, r0 + rows.stop)
            vb = env["v"][rows].astype(BF16)
            pc = slice(pr * pair, (pr + 1) * pair)
            scores = jnp.where(diag_mask, _dot_nt(ch["q_d"][:, pc], block_diag(ch["k_d"][:, pc])), 0.0)
            for ql, kl, mask in ch["lvl"]:
                scores = jnp.where(mask, _dot_nt(ql[:, pc], block_diag(kl[:, pc])), scores)
            st = state_ref[pr]
            o = (_dot(scores.astype(BF16), block_diag(vb[:, pc]))
                 + _dot_nt(ch["q_dec"][:, pc], st.astype(BF16)))
            upd = _dot_tn(vb[:, pc], ch["k_end"][:, pc])
            state_ref[pr] = st * ch["decay"][:, pc] + jnp.where(same_head, upd, 0.0)
            for hh in range(2):
                cols = slice(pr * pair + hh * GROUP, pr * pair + (hh + 1) * GROUP)
                oh = o[:, hh * GROUP:(hh + 1) * GROUP]
                oh = oh * lax.rsqrt(jnp.mean(oh * oh, -1, keepdims=True) + RMS_EPS) * hng[:, cols]
                yb_ref[out_rows, cols] = (oh * env["gate"][rows, cols]).astype(BF16)

        def gmlp_u():
            env["u"] = jax.nn.gelu(proj(0))

        def gmlp_v():
            v = jax.nn.gelu(proj(1))
            env["vnb"] = _layer_norm(v, gng_ref[...], gnb_ref[...]).astype(BF16)

        def gmlp_mix(groups):
            chunk_causal = (t_i // STREAM_CHUNK) >= (s_i // STREAM_CHUNK)
            u, vnb = env["u"], env["vnb"]
            for g in groups:
                cols = slice(g * GROUP, (g + 1) * GROUP)
                wg = jnp.where(chunk_causal, gw_ref[g], 0.0).astype(BF16)
                for blk in range(0, sub // GM_BLOCK, 2):
                    r_a = slice(blk * GM_BLOCK, (blk + 1) * GM_BLOCK)
                    r_b = slice((blk + 1) * GM_BLOCK, (blk + 2) * GM_BLOCK)
                    mixed = _dot(wg, jnp.concatenate([vnb[r_a, cols], vnb[r_b, cols]], axis=1))
                    ya_ref[r0 + r_a.start:r0 + r_a.stop, cols] = (
                        u[r_a, cols] * (mixed[:, :GROUP] + gbias_ref[:, cols])).astype(BF16)
                    ya_ref[r0 + r_b.start:r0 + r_b.stop, cols] = (
                        u[r_b, cols] * (mixed[:, GROUP:] + gbias_ref[:, cols])).astype(BF16)

        def gate_a():
            env["ga"] = jax.nn.sigmoid(proj(6))

        def branch_a():
            env["ma"] = env["ga"] * _dot(ya_ref[r0:r0 + sub, :], wa_ref[...])

        def gate_b():
            env["gb"] = jax.nn.sigmoid(proj(7))

        def merge():
            merged = env["ma"] + env["gb"] * _dot(yb_ref[r0:r0 + sub, :], wb_ref[...])
            env["mix"] = _dot(merged.astype(BF16), wo_ref[...])

        def ln1():
            o_ref[r0:r0 + sub, :] = _layer_norm(alpha * env["h"] + env["mix"], ln1g_ref[...], ln1b_ref[...])

        body = [hgrn_inputs]
        early = [proj_v, proj_gate]
        for c in range(n_chunks):
            body.append(functools.partial(chunk_prep, c))
            if early:
                body.append(early.pop(0))
        body += early
        fillers = [gmlp_u, gmlp_v,
                   functools.partial(gmlp_mix, range(0, n_groups // 2)),
                   functools.partial(gmlp_mix, range(n_groups // 2, n_groups)),
                   gate_a, branch_a, gate_b]
        for c in range(n_chunks):
            for pr in range(n_heads // 2):
                body.append(functools.partial(head_pair, c, pr))
                if fillers:
                    body.append(fillers.pop(0))
        body += fillers
        return ln0, body, [merge, ln1]

    parts = [sub_tile(r0) for r0 in range(0, tm, sub)]
    parts[0][0]()
    for i, (_, body, tail) in enumerate(parts):
        for piece in body:
            piece()
        if i + 1 < len(parts):
            parts[i + 1][0]()
            nxt = parts[i + 1][1]
            tail[0]()
            nxt.pop(0)()
            tail[1]()
        else:
            for piece in tail:
                piece()


def _ffn_kernel(h_ref, p_ref, wg_ref, wu_ref, wd_ref, wpg_ref, bpg_ref, wple_ref, ln2g_ref, ln2b_ref,
                o_ref, *, alpha, sub):
    def residual_sum(rows):
        h1 = h_ref[rows, :]
        hb = h1.astype(BF16)
        hid = jax.nn.silu(_dot(hb, wg_ref[...])) * _dot(hb, wu_ref[...])
        ffn = _dot(hid.astype(BF16), wd_ref[...])
        ple = (jax.nn.sigmoid(_dot(hb, wpg_ref[...]) + bpg_ref[...])
               * _dot(p_ref[rows, :].astype(BF16), wple_ref[...]))
        return alpha * h1 + ffn + ple

    row_blocks = [slice(r, r + sub) for r in range(0, h_ref.shape[0], sub)]
    sums = [residual_sum(rows) for rows in row_blocks]
    for rows, pre in zip(row_blocks, sums):
        o_ref[rows, :] = _layer_norm(pre, ln2g_ref[...], ln2b_ref[...])


def _resident(arr):
    nd = arr.ndim
    return pl.BlockSpec(arr.shape, lambda *_: (0,) * nd, pipeline_mode=pl.Buffered(1))


def kernel(x, p, ln0_g, ln0_b, w_in, b_in, gm_norm_g, gm_norm_b, gm_w_s, gm_b_s, hg_lb_logits, hg_norm_g, w_a, w_b, w_o, ln1_g, ln1_b, w_ffn_gate, w_ffn_up, w_ffn_down, w_ple, w_ple_gate, b_ple_gate, ln2_g, ln2_b):
    bsz, seq, d = x.shape
    depth = w_in.shape[0]
    assert depth == 1, "single-layer block"
    assert seq % TM_MIX == 0 and (bsz * seq) % TM_FFN == 0
    assert TM_MIX % SUB_MIX == 0 and SUB_MIX % (2 * HG_CHUNK) == 0 and TM_FFN % SUB_FFN == 0
    assert w_in.shape[2] == 8 * d and d % (2 * GROUP) == 0
    alpha = (2.0 * depth) ** 0.25
    layer = 0
    row = lambda a: a.reshape(1, -1).astype(F32)

    gbias = jnp.repeat(gm_b_s[layer].T, GROUP, axis=1)
    mix_consts = [row(ln0_g), row(ln0_b), w_in[layer].astype(BF16), row(b_in[layer]),
                  row(gm_norm_g[layer]), row(gm_norm_b[layer]), gm_w_s[layer], gbias,
                  hg_lb_logits.astype(F32), row(hg_norm_g[layer]),
                  w_a[layer].astype(BF16), w_b[layer].astype(BF16), w_o[layer].astype(BF16),
                  row(ln1_g[layer]), row(ln1_b[layer])]
    h1 = pl.pallas_call(
        functools.partial(_mix_kernel, tm=TM_MIX, sub=SUB_MIX, width=d, layer=layer, alpha=alpha),
        name="token_mixing",
        grid=(bsz, seq // TM_MIX),
        in_specs=[pl.BlockSpec((None, TM_MIX, d), lambda b, s: (b, s, 0))]
                 + [_resident(a) for a in mix_consts],
        out_specs=pl.BlockSpec((None, TM_MIX, d), lambda b, s: (b, s, 0)),
        out_shape=jax.ShapeDtypeStruct((bsz, seq, d), F32),
        scratch_shapes=[pltpu.VMEM((d // (2 * GROUP), 2 * GROUP, 2 * GROUP), F32),
                        pltpu.VMEM((TM_MIX, d), BF16),
                        pltpu.VMEM((TM_MIX, d), BF16)],
        compiler_params=pltpu.CompilerParams(
            dimension_semantics=("arbitrary", "arbitrary"),
            vmem_limit_bytes=V7X_VMEM_LIMIT_BYTES),
    )(x, *mix_consts)

    n_tok = bsz * seq
    ple_dim = p.shape[-1]
    ffn_consts = [w_ffn_gate[layer].astype(BF16), w_ffn_up[layer].astype(BF16),
                  w_ffn_down[layer].astype(BF16), w_ple_gate[layer].astype(BF16),
                  row(b_ple_gate[layer]), w_ple[layer].astype(BF16),
                  row(ln2_g[layer]), row(ln2_b[layer])]
    out = pl.pallas_call(
        functools.partial(_ffn_kernel, alpha=alpha, sub=SUB_FFN),
        name="channel_mixing",
        grid=(n_tok // TM_FFN,),
        in_specs=[pl.BlockSpec((TM_FFN, d), lambda i: (i, 0)),
                  pl.BlockSpec((TM_FFN, ple_dim), lambda i: (i, 0))]
                 + [_resident(a) for a in ffn_consts],
        out_specs=pl.BlockSpec((TM_FFN, d), lambda i: (i, 0)),
        out_shape=jax.ShapeDtypeStruct((n_tok, d), F32),
        compiler_params=pltpu.CompilerParams(
            dimension_semantics=("arbitrary",),
            vmem_limit_bytes=V7X_VMEM_LIMIT_BYTES),
    )(h1.reshape(n_tok, d), p[layer].reshape(n_tok, ple_dim), *ffn_consts)
    return out.reshape(bsz, seq, d)
```

```python
import functools

import jax
import jax.numpy as jnp
from jax import lax
from jax.experimental import pallas as pl
from jax.experimental.pallas import tpu as pltpu

F32 = jnp.float32
BF16 = jnp.bfloat16

LN_EPS = 1e-5
RMS_EPS = 1e-6
STREAM_CHUNK = 64
GM_BLOCK = 128
GROUP = 128
HG_CHUNK = 128
HG_DIAG = 16
V7X_VMEM_LIMIT_BYTES = 56 * 1024 * 1024

TM_MIX = 512
SUB_MIX = 256
TM_FFN = 512
SUB_FFN = 256


def _layer_norm(x, g, b):
    mu = jnp.mean(x, -1, keepdims=True)
    xc = x - mu
    var = jnp.mean(xc * xc, -1, keepdims=True)
    return xc * lax.rsqrt(var + LN_EPS) * g + b


def _dot(a, b):
    return jnp.dot(a, b, preferred_element_type=F32)


def _dot_nt(a, b):
    return lax.dot_general(a, b, (((1,), (1,)), ((), ())), preferred_element_type=F32)


def _dot_tn(a, b):
    return lax.dot_general(a, b, (((0,), (0,)), ((), ())), preferred_element_type=F32)


def _split3(x):
    hi = x.astype(BF16)
    r1 = x - hi.astype(F32)
    mid = r1.astype(BF16)
    lo = (r1 - mid.astype(F32)).astype(BF16)
    return hi, mid, lo


def _row_ref(b, period, row):
    parts = []
    for j in range(HG_CHUNK // period):
        r = j * period + row
        parts.append(jnp.broadcast_to(b[r:r + 1, :], (period, b.shape[1])))
    return parts[0] if len(parts) == 1 else jnp.concatenate(parts, axis=0)


def _mix_kernel(x_ref, ln0g_ref, ln0b_ref, win_ref, bin_ref, gng_ref, gnb_ref, gw_ref, gbias_ref,
                lbl_ref, hng_ref, wa_ref, wb_ref, wo_ref, ln1g_ref, ln1b_ref,
                o_ref, state_ref, ya_ref, yb_ref, *, tm, sub, width, layer, alpha):
    n_heads = width // GROUP
    n_groups = width // GROUP
    n_chunks = sub // HG_CHUNK
    pair = 2 * GROUP

    @pl.when(pl.program_id(1) == 0)
    def _():
        state_ref[...] = jnp.zeros_like(state_ref)

    t_i = lax.broadcasted_iota(jnp.int32, (GM_BLOCK, GM_BLOCK), 0)
    s_i = lax.broadcasted_iota(jnp.int32, (GM_BLOCK, GM_BLOCK), 1)
    logits = lbl_ref[...]
    e_l = jnp.exp(logits - jnp.max(logits, axis=0, keepdims=True))
    lb = jnp.sum(e_l[:layer + 1], axis=0, keepdims=True) / jnp.sum(e_l, axis=0, keepdims=True)
    hng = hng_ref[...]
    cum_mat = jnp.where(t_i >= s_i, 1.0, 0.0).astype(BF16)
    cum_mat3 = jnp.concatenate([cum_mat, cum_mat, cum_mat], axis=1)
    t_p = lax.broadcasted_iota(jnp.int32, (HG_CHUNK, pair), 0)
    s_p = lax.broadcasted_iota(jnp.int32, (HG_CHUNK, pair), 1) % GROUP
    level_masks = []
    m = HG_CHUNK // 2
    while m >= HG_DIAG:
        level_masks.append((m, (t_p // (2 * m) == s_p // (2 * m)) & (t_p % (2 * m) >= m)
                            & (s_p % (2 * m) < m)))
        m //= 2
    diag_mask = (t_p // HG_DIAG == s_p // HG_DIAG) & (t_p >= s_p)
    same_head = (lax.broadcasted_iota(jnp.int32, (pair, pair), 0) // GROUP
                 == lax.broadcasted_iota(jnp.int32, (pair, pair), 1) // GROUP)
    zero_blk = jnp.zeros((HG_CHUNK, GROUP), BF16)

    def block_diag(xp):
        return jnp.concatenate([jnp.concatenate([xp[:, :GROUP], zero_blk], axis=1),
                                jnp.concatenate([zero_blk, xp[:, GROUP:]], axis=1)], axis=0)

    def sub_tile(r0):
        env = {}

        def ln0():
            h = _layer_norm(x_ref[r0:r0 + sub, :], ln0g_ref[...], ln0b_ref[...])
            env["h"] = h
            env["hb"] = h.astype(BF16)

        def proj(sec):
            cols = slice(sec * width, (sec + 1) * width)
            return _dot(env["hb"], win_ref[:, cols]) + bin_ref[:, cols]

        def hgrn_inputs():
            sig_f = jax.nn.sigmoid(proj(3))
            env["logf"] = jnp.log2(lb + (1.0 - lb) * sig_f)
            env["k"] = (1.0 - lb) * (1.0 - sig_f)
            env["q"] = jax.nn.silu(proj(2))

        def chunk_prep(c):
            rows = slice(c * HG_CHUNK, (c + 1) * HG_CHUNK)
            q, k, logf = env["q"][rows], env["k"][rows], env["logf"][rows]
            b = _dot(cum_mat3, jnp.concatenate(_split3(logf), axis=0))
            b_last = b[HG_CHUNK - 1:HG_CHUNK, :]
            ch = {"rows": rows}
            qb, kb = q.astype(BF16), k.astype(BF16)
            ch["q_dec"] = qb * jnp.exp2(b).astype(BF16)
            ch["k_end"] = kb * jnp.exp2(b_last - b).astype(BF16)
            ch["decay"] = jnp.exp2(b_last)
            lvl = []
            for m, mask in level_masks:
                e = jnp.exp2(-jnp.abs(b - _row_ref(b, 2 * m, m - 1))).astype(BF16)
                lvl.append((qb * e, kb * e, mask))
            ch["lvl"] = lvl
            dq = b - (_row_ref(b, HG_DIAG, 0) - _row_ref(logf, HG_DIAG, 0))
            ch["q_d"] = qb * jnp.exp2(dq).astype(BF16)
            ch["k_d"] = kb * jnp.exp2(-dq).astype(BF16)
            env["chunk", c] = ch

        def proj_v():
            env["v"] = proj(4).astype(BF16)

        def proj_gate():
            env["gate"] = jax.nn.silu(proj(5))

        def head_pair(c, pr):
            ch = env["chunk", c]
            rows = ch["rows"]
            out_rows = slice(r0 + rows.start, r0 + rows.stop)
            vb = env["v"][rows]
            pc = slice(pr * pair, (pr + 1) * pair)
            scores = jnp.where(diag_mask, _dot_nt(ch["q_d"][:, pc], block_diag(ch["k_d"][:, pc])), 0.0)
            for ql, kl, mask in ch["lvl"]:
                scores = jnp.where(mask, _dot_nt(ql[:, pc], block_diag(kl[:, pc])), scores)
            st = state_ref[pr]
            o = (_dot(scores.astype(BF16), block_diag(vb[:, pc]))
                 + _dot_nt(ch["q_dec"][:, pc], st.astype(BF16)))
            upd = _dot_tn(vb[:, pc], ch["k_end"][:, pc])
            state_ref[pr] = st * ch["decay"][:, pc] + jnp.where(same_head, upd, 0.0)
            for hh in range(2):
                cols = slice(pr * pair + hh * GROUP, pr * pair + (hh + 1) * GROUP)
                oh = o[:, hh * GROUP:(hh + 1) * GROUP]
                oh = oh * lax.rsqrt(jnp.mean(oh * oh, -1, keepdims=True) + RMS_EPS) * hng[:, cols]
                yb_ref[out_rows, cols] = (oh * env["gate"][rows, cols]).astype(BF16)

        def gmlp_u():
            env["u"] = jax.nn.gelu(proj(0))

        def gmlp_v():
            v = jax.nn.gelu(proj(1))
            env["vnb"] = _layer_norm(v, gng_ref[...], gnb_ref[...]).astype(BF16)

        def gmlp_mix(groups):
            chunk_causal = (t_i // STREAM_CHUNK) >= (s_i // STREAM_CHUNK)
            u, vnb = env["u"], env["vnb"]
            for g in groups:
                cols = slice(g * GROUP, (g + 1) * GROUP)
                wg = jnp.where(chunk_causal, gw_ref[g], 0.0).astype(BF16)
                for blk in range(0, sub // GM_BLOCK, 2):
                    r_a = slice(blk * GM_BLOCK, (blk + 1) * GM_BLOCK)
                    r_b = slice((blk + 1) * GM_BLOCK, (blk + 2) * GM_BLOCK)
                    mixed = _dot(wg, jnp.concatenate([vnb[r_a, cols], vnb[r_b, cols]], axis=1))
                    ya_ref[r0 + r_a.start:r0 + r_a.stop, cols] = (
                        u[r_a, cols] * (mixed[:, :GROUP] + gbias_ref[:, cols])).astype(BF16)
                    ya_ref[r0 + r_b.start:r0 + r_b.stop, cols] = (
                        u[r_b, cols] * (mixed[:, GROUP:] + gbias_ref[:, cols])).astype(BF16)

        def gate_a():
            env["ga"] = jax.nn.sigmoid(proj(6))

        def branch_a():
            env["ma"] = env["ga"] * _dot(ya_ref[r0:r0 + sub, :], wa_ref[...])

        def gate_b():
            env["gb"] = jax.nn.sigmoid(proj(7))

        def merge():
            merged = env["ma"] + env["gb"] * _dot(yb_ref[r0:r0 + sub, :], wb_ref[...])
            env["mix"] = _dot(merged.astype(BF16), wo_ref[...])

        def ln1():
            o_ref[r0:r0 + sub, :] = _layer_norm(alpha * env["h"] + env["mix"], ln1g_ref[...], ln1b_ref[...])

        body = [hgrn_inputs]
        early = [proj_v, proj_gate]
        for c in range(n_chunks):
            body.append(functools.partial(chunk_prep, c))
            if early:
                body.append(early.pop(0))
        body += early
        fillers = [gmlp_u, gmlp_v,
                   functools.partial(gmlp_mix, range(0, n_groups // 2)),
                   functools.partial(gmlp_mix, range(n_groups // 2, n_groups)),
                   gate_a, branch_a, gate_b]
        for c in range(n_chunks):
            for pr in range(n_heads // 2):
                body.append(functools.partial(head_pair, c, pr))
                if fillers:
                    body.append(fillers.pop(0))
        body += fillers
        return ln0, body, [merge, ln1]

    parts = [sub_tile(r0) for r0 in range(0, tm, sub)]
    parts[0][0]()
    for i, (_, body, tail) in enumerate(parts):
        for piece in body:
            piece()
        if i + 1 < len(parts):
            parts[i + 1][0]()
            nxt = parts[i + 1][1]
            tail[0]()
            nxt.pop(0)()
            tail[1]()
        else:
            for piece in tail:
                piece()


def _ffn_kernel(h_ref, p_ref, wg_ref, wu_ref, wd_ref, wpg_ref, bpg_ref, wple_ref, ln2g_ref, ln2b_ref,
                o_ref, *, alpha, sub):
    def residual_sum(rows):
        h1 = h_ref[rows, :]
        hb = h1.astype(BF16)
        hid = jax.nn.silu(_dot(hb, wg_ref[...])) * _dot(hb, wu_ref[...])
        ffn = _dot(hid.astype(BF16), wd_ref[...])
        ple = (jax.nn.sigmoid(_dot(hb, wpg_ref[...]) + bpg_ref[...])
               * _dot(p_ref[rows, :].astype(BF16), wple_ref[...]))
        return alpha * h1 + ffn + ple

    row_blocks = [slice(r, r + sub) for r in range(0, h_ref.shape[0], sub)]
    sums = [residual_sum(rows) for rows in row_blocks]
    for rows, pre in zip(row_blocks, sums):
        o_ref[rows, :] = _layer_norm(pre, ln2g_ref[...], ln2b_ref[...])


def _resident(arr):
    nd = arr.ndim
    return pl.BlockSpec(arr.shape, lambda *_: (0,) * nd, pipeline_mode=pl.Buffered(1))


def kernel(x, p, ln0_g, ln0_b, w_in, b_in, gm_norm_g, gm_norm_b, gm_w_s, gm_b_s, hg_lb_logits, hg_norm_g, w_a, w_b, w_o, ln1_g, ln1_b, w_ffn_gate, w_ffn_up, w_ffn_down, w_ple, w_ple_gate, b_ple_gate, ln2_g, ln2_b):
    bsz, seq, d = x.shape
    depth = w_in.shape[0]
    assert depth == 1, "single-layer block"
    assert seq % TM_MIX == 0 and (bsz * seq) % TM_FFN == 0
    assert TM_MIX % SUB_MIX == 0 and SUB_MIX % (2 * HG_CHUNK) == 0 and TM_FFN % SUB_FFN == 0
    assert w_in.shape[2] == 8 * d and d % (2 * GROUP) == 0
    alpha = (2.0 * depth) ** 0.25
    layer = 0
    row = lambda a: a.reshape(1, -1).astype(F32)

    gbias = jnp.repeat(gm_b_s[layer].T, GROUP, axis=1)
    mix_consts = [row(ln0_g), row(ln0_b), w_in[layer].astype(BF16), row(b_in[layer]),
                  row(gm_norm_g[layer]), row(gm_norm_b[layer]), gm_w_s[layer], gbias,
                  hg_lb_logits.astype(F32), row(hg_norm_g[layer]),
                  w_a[layer].astype(BF16), w_b[layer].astype(BF16), w_o[layer].astype(BF16),
                  row(ln1_g[layer]), row(ln1_b[layer])]
    h1 = pl.pallas_call(
        functools.partial(_mix_kernel, tm=TM_MIX, sub=SUB_MIX, width=d, layer=layer, alpha=alpha),
        name="token_mixing",
        grid=(bsz, seq // TM_MIX),
        in_specs=[pl.BlockSpec((None, TM_MIX, d), lambda b, s: (b, s, 0))]
                 + [_resident(a) for a in mix_consts],
        out_specs=pl.BlockSpec((None, TM_MIX, d), lambda b, s: (b, s, 0)),
        out_shape=jax.ShapeDtypeStruct((bsz, seq, d), F32),
        scratch_shapes=[pltpu.VMEM((d // (2 * GROUP), 2 * GROUP, 2 * GROUP), F32),
                        pltpu.VMEM((TM_MIX, d), BF16),
                        pltpu.VMEM((TM_MIX, d), BF16)],
        compiler_params=pltpu.CompilerParams(
            dimension_semantics=("arbitrary", "arbitrary"),
            vmem_limit_bytes=V7X_VMEM_LIMIT_BYTES),
    )(x, *mix_consts)

    n_tok = bsz * seq
    ple_dim = p.shape[-1]
    ffn_consts = [w_ffn_gate[layer].astype(BF16), w_ffn_up[layer].astype(BF16),
                  w_ffn_down[layer].astype(BF16), w_ple_gate[layer].astype(BF16),
                  row(b_ple_gate[layer]), w_ple[layer].astype(BF16),
                  row(ln2_g[layer]), row(ln2_b[layer])]
    out = pl.pallas_call(
        functools.partial(_ffn_kernel, alpha=alpha, sub=SUB_FFN),
        name="channel_mixing",
        grid=(n_tok // TM_FFN,),
        in_specs=[pl.BlockSpec((TM_FFN, d), lambda i: (i, 0)),
                  pl.BlockSpec((TM_FFN, ple_dim), lambda i: (i, 0))]
                 + [_resident(a) for a in ffn_consts],
        out_specs=pl.BlockSpec((TM_FFN, d), lambda i: (i, 0)),
        out_shape=jax.ShapeDtypeStruct((n_tok, d), F32),
        compiler_params=pltpu.CompilerParams(
            dimension_semantics=("arbitrary",),
            vmem_limit_bytes=V7X_VMEM_LIMIT_BYTES),
    )(h1.reshape(n_tok, d), p[layer].reshape(n_tok, ple_dim), *ffn_consts)
    return out.reshape(bsz, seq, d)
```

```python
import functools

import jax
import jax.numpy as jnp
from jax import lax
from jax.experimental import pallas as pl
from jax.experimental.pallas import tpu as pltpu

F32 = jnp.float32
BF16 = jnp.bfloat16

LN_EPS = 1e-5
RMS_EPS = 1e-6
STREAM_CHUNK = 64
GM_BLOCK = 128
GROUP = 128
HG_CHUNK = 128
HG_DIAG = 16
V7X_VMEM_LIMIT_BYTES = 56 * 1024 * 1024

TM_MIX = 512
SUB_MIX = 256
TM_FFN = 512
SUB_FFN = 256


def _layer_norm(x, g, b):
    mu = jnp.mean(x, -1, keepdims=True)
    xc = x - mu
    var = jnp.mean(xc * xc, -1, keepdims=True)
    return xc * lax.rsqrt(var + LN_EPS) * g + b


def _dot(a, b):
    return jnp.dot(a, b, preferred_element_type=F32)


def _dot_nt(a, b):
    return lax.dot_general(a, b, (((1,), (1,)), ((), ())), preferred_element_type=F32)


def _dot_tn(a, b):
    return lax.dot_general(a, b, (((0,), (0,)), ((), ())), preferred_element_type=F32)


def _split3(x):
    hi = x.astype(BF16)
    r1 = x - hi.astype(F32)
    mid = r1.astype(BF16)
    lo = (r1 - mid.astype(F32)).astype(BF16)
    return hi, mid, lo


def _row_ref(b, period, row):
    parts = []
    for j in range(HG_CHUNK // period):
        r = j * period + row
        parts.append(jnp.broadcast_to(b[r:r + 1, :], (period, b.shape[1])))
    return parts[0] if len(parts) == 1 else jnp.concatenate(parts, axis=0)


def _mix_kernel(x_ref, ln0g_ref, ln0b_ref, win_ref, bin_ref, gng_ref, gnb_ref, gw_ref, gbias_ref,
                lbl_ref, hng_ref, wa_ref, wb_ref, wo_ref, ln1g_ref, ln1b_ref,
                o_ref, state_ref, ya_ref, yb_ref, *, tm, sub, width, layer, alpha):
    n_heads = width // GROUP
    n_groups = width // GROUP
    n_chunks = sub // HG_CHUNK
    pair = 2 * GROUP

    @pl.when(pl.program_id(1) == 0)
    def _():
        state_ref[...] = jnp.zeros_like(state_ref)

    t_i = lax.broadcasted_iota(jnp.int32, (GM_BLOCK, GM_BLOCK), 0)
    s_i = lax.broadcasted_iota(jnp.int32, (GM_BLOCK, GM_BLOCK), 1)
    logits = lbl_ref[...]
    e_l = jnp.exp(logits - jnp.max(logits, axis=0, keepdims=True))
    lb = jnp.sum(e_l[:layer + 1], axis=0, keepdims=True) / jnp.sum(e_l, axis=0, keepdims=True)
    hng = hng_ref[...]
    cum_mat = jnp.where(t_i >= s_i, 1.0, 0.0).astype(BF16)
    cum_mat3 = jnp.concatenate([cum_mat, cum_mat, cum_mat], axis=1)
    t_p = lax.broadcasted_iota(jnp.int32, (HG_CHUNK, pair), 0)
    s_p = lax.broadcasted_iota(jnp.int32, (HG_CHUNK, pair), 1) % GROUP
    level_masks = []
    m = HG_CHUNK // 2
    while m >= HG_DIAG:
        level_masks.append((m, (t_p // (2 * m) == s_p // (2 * m)) & (t_p % (2 * m) >= m)
                            & (s_p % (2 * m) < m)))
        m //= 2
    diag_mask = (t_p // HG_DIAG == s_p // HG_DIAG) & (t_p >= s_p)
    same_head = (lax.broadcasted_iota(jnp.int32, (pair, pair), 0) // GROUP
                 == lax.broadcasted_iota(jnp.int32, (pair, pair), 1) // GROUP)
    zero_blk = jnp.zeros((HG_CHUNK, GROUP), BF16)

    def block_diag(xp):
        return jnp.concatenate([jnp.concatenate([xp[:, :GROUP], zero_blk], axis=1),
                                jnp.concatenate([zero_blk, xp[:, GROUP:]], axis=1)], axis=0)

    def sub_tile(r0):
        env = {}

        def ln0():
            h = _layer_norm(x_ref[r0:r0 + sub, :], ln0g_ref[...], ln0b_ref[...])
            env["h"] = h
            env["hb"] = h.astype(BF16)

        def proj(sec):
            cols = slice(sec * width, (sec + 1) * width)
            return _dot(env["hb"], win_ref[:, cols]) + bin_ref[:, cols]

        def hgrn_inputs():
            sig_f = jax.nn.sigmoid(proj(3))
            env["logf"] = jnp.log2(lb + (1.0 - lb) * sig_f)
            env["k"] = (1.0 - lb) * (1.0 - sig_f)
            env["q"] = jax.nn.silu(proj(2))

        def chunk_prep(c):
            rows = slice(c * HG_CHUNK, (c + 1) * HG_CHUNK)
            q, k, logf = env["q"][rows], env["k"][rows], env["logf"][rows]
            b = _dot(cum_mat3, jnp.concatenate(_split3(logf), axis=0))
            b_last = b[HG_CHUNK - 1:HG_CHUNK, :]
            ch = {"rows": rows}
            qb, kb = q.astype(BF16), k.astype(BF16)
            ch["q_dec"] = qb * jnp.exp2(b).astype(BF16)
            ch["k_end"] = kb * jnp.exp2(b_last - b).astype(BF16)
            ch["decay"] = jnp.exp2(b_last)
            lvl = []
            for m, mask in level_masks:
                e = jnp.exp2(-jnp.abs(b - _row_ref(b, 2 * m, m - 1))).astype(BF16)
                lvl.append((qb * e, kb * e, mask))
            ch["lvl"] = lvl
            dq = b - (_row_ref(b, HG_DIAG, 0) - _row_ref(logf, HG_DIAG, 0))
            ch["q_d"] = qb * jnp.exp2(dq).astype(BF16)
            ch["k_d"] = kb * jnp.exp2(-dq).astype(BF16)
            env["chunk", c] = ch

        def proj_v():
            env["v"] = proj(4).astype(BF16)

        def proj_gate():
            env["gate"] = jax.nn.silu(proj(5)).astype(BF16)

        def head_pair(c, pr):
            ch = env["chunk", c]
            rows = ch["rows"]
            out_rows = slice(r0 + rows.start, r0 + rows.stop)
            vb = env["v"][rows]
            pc = slice(pr * pair, (pr + 1) * pair)
            scores = jnp.where(diag_mask, _dot_nt(ch["q_d"][:, pc], block_diag(ch["k_d"][:, pc])), 0.0)
            for ql, kl, mask in ch["lvl"]:
                scores = jnp.where(mask, _dot_nt(ql[:, pc], block_diag(kl[:, pc])), scores)
            st = state_ref[pr]
            o = (_dot(scores.astype(BF16), block_diag(vb[:, pc]))
                 + _dot_nt(ch["q_dec"][:, pc], st.astype(BF16)))
            upd = _dot_tn(vb[:, pc], ch["k_end"][:, pc])
            state_ref[pr] = st * ch["decay"][:, pc] + jnp.where(same_head, upd, 0.0)
            for hh in range(2):
                cols = slice(pr * pair + hh * GROUP, pr * pair + (hh + 1) * GROUP)
                oh = o[:, hh * GROUP:(hh + 1) * GROUP]
                oh = oh * lax.rsqrt(jnp.mean(oh * oh, -1, keepdims=True) + RMS_EPS) * hng[:, cols]
                yb_ref[out_rows, cols] = oh.astype(BF16) * env["gate"][rows, cols]

        def gmlp_u():
            env["u"] = jax.nn.gelu(proj(0)).astype(BF16)

        def gmlp_v():
            v = jax.nn.gelu(proj(1))
            env["vnb"] = _layer_norm(v, gng_ref[...], gnb_ref[...]).astype(BF16)

        def gmlp_mix(groups):
            chunk_causal = (t_i // STREAM_CHUNK) >= (s_i // STREAM_CHUNK)
            u, vnb = env["u"], env["vnb"]
            for g in groups:
                cols = slice(g * GROUP, (g + 1) * GROUP)
                wg = jnp.where(chunk_causal, gw_ref[g], 0.0).astype(BF16)
                for blk in range(0, sub // GM_BLOCK, 2):
                    r_a = slice(blk * GM_BLOCK, (blk + 1) * GM_BLOCK)
                    r_b = slice((blk + 1) * GM_BLOCK, (blk + 2) * GM_BLOCK)
                    mixed = _dot(wg, jnp.concatenate([vnb[r_a, cols], vnb[r_b, cols]], axis=1))
                    ya_ref[r0 + r_a.start:r0 + r_a.stop, cols] = (
                        u[r_a, cols] * (mixed[:, :GROUP] + gbias_ref[:, cols]).astype(BF16))
                    ya_ref[r0 + r_b.start:r0 + r_b.stop, cols] = (
                        u[r_b, cols] * (mixed[:, GROUP:] + gbias_ref[:, cols]).astype(BF16))

        def gate_a():
            env["ga"] = jax.nn.sigmoid(proj(6))

        def branch_a():
            env["ma"] = env["ga"] * _dot(ya_ref[r0:r0 + sub, :], wa_ref[...])

        def gate_b():
            env["gb"] = jax.nn.sigmoid(proj(7))

        def merge():
            merged = env["ma"] + env["gb"] * _dot(yb_ref[r0:r0 + sub, :], wb_ref[...])
            env["mix"] = _dot(merged.astype(BF16), wo_ref[...])

        def ln1():
            o_ref[r0:r0 + sub, :] = _layer_norm(alpha * env["h"] + env["mix"], ln1g_ref[...], ln1b_ref[...])

        body = [hgrn_inputs, gmlp_u, gmlp_v]
        early = [proj_v, proj_gate]
        for c in range(n_chunks):
            body.append(functools.partial(chunk_prep, c))
            if early:
                body.append(early.pop(0))
        body += early
        body += [functools.partial(gmlp_mix, range(0, n_groups // 2)),
                 functools.partial(gmlp_mix, range(n_groups // 2, n_groups))]
        fillers = [gate_a, branch_a, gate_b]
        units = [(c, pr) for c in range(n_chunks) for pr in range(n_heads // 2)]
        stride = max(1, len(units) // (len(fillers) + 1))
        for i, (c, pr) in enumerate(units):
            body.append(functools.partial(head_pair, c, pr))
            if i % stride == stride - 1 and fillers:
                body.append(fillers.pop(0))
        body += fillers
        return ln0, body, [merge, ln1]

    parts = [sub_tile(r0) for r0 in range(0, tm, sub)]
    parts[0][0]()
    for i, (_, body, tail) in enumerate(parts):
        for piece in body:
            piece()
        if i + 1 < len(parts):
            parts[i + 1][0]()
            nxt = parts[i + 1][1]
            tail[0]()
            nxt.pop(0)()
            tail[1]()
        else:
            for piece in tail:
                piece()


def _ffn_kernel(h_ref, p_ref, wg_ref, wu_ref, wd_ref, wpg_ref, bpg_ref, wple_ref, ln2g_ref, ln2b_ref,
                o_ref, *, alpha, sub):
    def residual_sum(rows):
        h1 = h_ref[rows, :]
        hb = h1.astype(BF16)
        hid = jax.nn.silu(_dot(hb, wg_ref[...])) * _dot(hb, wu_ref[...])
        ffn = _dot(hid.astype(BF16), wd_ref[...])
        ple = (jax.nn.sigmoid(_dot(hb, wpg_ref[...]) + bpg_ref[...])
               * _dot(p_ref[rows, :].astype(BF16), wple_ref[...]))
        return alpha * h1 + ffn + ple

    row_blocks = [slice(r, r + sub) for r in range(0, h_ref.shape[0], sub)]
    sums = [residual_sum(rows) for rows in row_blocks]
    for rows, pre in zip(row_blocks, sums):
        o_ref[rows, :] = _layer_norm(pre, ln2g_ref[...], ln2b_ref[...])


def _resident(arr):
    nd = arr.ndim
    return pl.BlockSpec(arr.shape, lambda *_: (0,) * nd, pipeline_mode=pl.Buffered(1))


def kernel(x, p, ln0_g, ln0_b, w_in, b_in, gm_norm_g, gm_norm_b, gm_w_s, gm_b_s, hg_lb_logits, hg_norm_g, w_a, w_b, w_o, ln1_g, ln1_b, w_ffn_gate, w_ffn_up, w_ffn_down, w_ple, w_ple_gate, b_ple_gate, ln2_g, ln2_b):
    bsz, seq, d = x.shape
    depth = w_in.shape[0]
    assert depth == 1, "single-layer block"
    assert seq % TM_MIX == 0 and (bsz * seq) % TM_FFN == 0
    assert TM_MIX % SUB_MIX == 0 and SUB_MIX % (2 * HG_CHUNK) == 0 and TM_FFN % SUB_FFN == 0
    assert w_in.shape[2] == 8 * d and d % (2 * GROUP) == 0
    alpha = (2.0 * depth) ** 0.25
    layer = 0
    row = lambda a: a.reshape(1, -1).astype(F32)

    gbias = jnp.repeat(gm_b_s[layer].T, GROUP, axis=1)
    mix_consts = [row(ln0_g), row(ln0_b), w_in[layer].astype(BF16), row(b_in[layer]),
                  row(gm_norm_g[layer]), row(gm_norm_b[layer]), gm_w_s[layer], gbias,
                  hg_lb_logits.astype(F32), row(hg_norm_g[layer]),
                  w_a[layer].astype(BF16), w_b[layer].astype(BF16), w_o[layer].astype(BF16),
                  row(ln1_g[layer]), row(ln1_b[layer])]
    h1 = pl.pallas_call(
        functools.partial(_mix_kernel, tm=TM_MIX, sub=SUB_MIX, width=d, layer=layer, alpha=alpha),
        name="token_mixing",
        grid=(bsz, seq // TM_MIX),
        in_specs=[pl.BlockSpec((None, TM_MIX, d), lambda b, s: (b, s, 0))]
                 + [_resident(a) for a in mix_consts],
        out_specs=pl.BlockSpec((None, TM_MIX, d), lambda b, s: (b, s, 0)),
        out_shape=jax.ShapeDtypeStruct((bsz, seq, d), F32),
        scratch_shapes=[pltpu.VMEM((d // (2 * GROUP), 2 * GROUP, 2 * GROUP), F32),
                        pltpu.VMEM((TM_MIX, d), BF16),
                        pltpu.VMEM((TM_MIX, d), BF16)],
        compiler_params=pltpu.CompilerParams(
            dimension_semantics=("arbitrary", "arbitrary"),
            vmem_limit_bytes=V7X_VMEM_LIMIT_BYTES),
    )(x, *mix_consts)

    n_tok = bsz * seq
    ple_dim = p.shape[-1]
    ffn_consts = [w_ffn_gate[layer].astype(BF16), w_ffn_up[layer].astype(BF16),
                  w_ffn_down[layer].astype(BF16), w_ple_gate[layer].astype(BF16),
                  row(b_ple_gate[layer]), w_ple[layer].astype(BF16),
                  row(ln2_g[layer]), row(ln2_b[layer])]
    out = pl.pallas_call(
        functools.partial(_ffn_kernel, alpha=alpha, sub=SUB_FFN),
        name="channel_mixing",
        grid=(n_tok // TM_FFN,),
        in_specs=[pl.BlockSpec((TM_FFN, d), lambda i: (i, 0)),
                  pl.BlockSpec((TM_FFN, ple_dim), lambda i: (i, 0))]
                 + [_resident(a) for a in ffn_consts],
        out_specs=pl.BlockSpec((TM_FFN, d), lambda i: (i, 0)),
        out_shape=jax.ShapeDtypeStruct((n_tok, d), F32),
        compiler_params=pltpu.CompilerParams(
            dimension_semantics=("arbitrary",),
            vmem_limit_bytes=V7X_VMEM_LIMIT_BYTES),
    )(h1.reshape(n_tok, d), p[layer].reshape(n_tok, ple_dim), *ffn_consts)
    return out.reshape(bsz, seq, d)
```

```python
import functools

import jax
import jax.numpy as jnp
from jax import lax
from jax.experimental import pallas as pl
from jax.experimental.pallas import tpu as pltpu

F32 = jnp.float32
BF16 = jnp.bfloat16

LN_EPS = 1e-5
RMS_EPS = 1e-6
STREAM_CHUNK = 64
GM_BLOCK = 128
GROUP = 128
HG_CHUNK = 128
HG_DIAG = 16
V7X_VMEM_LIMIT_BYTES = 56 * 1024 * 1024

TM_MIX = 512
SUB_MIX = 256
TM_FFN = 512
SUB_FFN = 256


def _layer_norm(x, g, b):
    mu = jnp.mean(x, -1, keepdims=True)
    xc = x - mu
    var = jnp.mean(xc * xc, -1, keepdims=True)
    return xc * lax.rsqrt(var + LN_EPS) * g + b


def _dot(a, b):
    return jnp.dot(a, b, preferred_element_type=F32)


def _dot_nt(a, b):
    return lax.dot_general(a, b, (((1,), (1,)), ((), ())), preferred_element_type=F32)


def _dot_tn(a, b):
    return lax.dot_general(a, b, (((0,), (0,)), ((), ())), preferred_element_type=F32)


def _split3(x):
    hi = x.astype(BF16)
    r1 = x - hi.astype(F32)
    mid = r1.astype(BF16)
    lo = (r1 - mid.astype(F32)).astype(BF16)
    return hi, mid, lo


def _row_ref(b, period, row):
    parts = []
    for j in range(HG_CHUNK // period):
        r = j * period + row
        parts.append(jnp.broadcast_to(b[r:r + 1, :], (period, b.shape[1])))
    return parts[0] if len(parts) == 1 else jnp.concatenate(parts, axis=0)


def _mix_kernel(x_ref, ln0g_ref, ln0b_ref, win_ref, bin_ref, gng_ref, gnb_ref, gw_ref, gbias_ref,
                lbl_ref, hng_ref, wa_ref, wb_ref, wo_ref, ln1g_ref, ln1b_ref,
                o_ref, state_ref, ya_ref, yb_ref, *, tm, sub, width, layer, alpha):
    n_heads = width // GROUP
    n_groups = width // GROUP
    n_chunks = sub // HG_CHUNK
    pair = 2 * GROUP

    @pl.when(pl.program_id(1) == 0)
    def _():
        state_ref[...] = jnp.zeros_like(state_ref)

    t_i = lax.broadcasted_iota(jnp.int32, (GM_BLOCK, GM_BLOCK), 0)
    s_i = lax.broadcasted_iota(jnp.int32, (GM_BLOCK, GM_BLOCK), 1)
    logits = lbl_ref[...]
    e_l = jnp.exp(logits - jnp.max(logits, axis=0, keepdims=True))
    lb = jnp.sum(e_l[:layer + 1], axis=0, keepdims=True) / jnp.sum(e_l, axis=0, keepdims=True)
    hng = hng_ref[...]
    cum_mat = jnp.where(t_i >= s_i, 1.0, 0.0).astype(BF16)
    cum_mat3 = jnp.concatenate([cum_mat, cum_mat, cum_mat], axis=1)
    t_p = lax.broadcasted_iota(jnp.int32, (HG_CHUNK, pair), 0)
    s_p = lax.broadcasted_iota(jnp.int32, (HG_CHUNK, pair), 1) % GROUP
    level_masks = []
    m = HG_CHUNK // 2
    while m >= HG_DIAG:
        level_masks.append((m, (t_p // (2 * m) == s_p // (2 * m)) & (t_p % (2 * m) >= m)
                            & (s_p % (2 * m) < m)))
        m //= 2
    diag_mask = (t_p // HG_DIAG == s_p // HG_DIAG) & (t_p >= s_p)
    same_head = (lax.broadcasted_iota(jnp.int32, (pair, pair), 0) // GROUP
                 == lax.broadcasted_iota(jnp.int32, (pair, pair), 1) // GROUP)
    zero_blk = jnp.zeros((HG_CHUNK, GROUP), BF16)

    def block_diag(xp):
        return jnp.concatenate([jnp.concatenate([xp[:, :GROUP], zero_blk], axis=1),
                                jnp.concatenate([zero_blk, xp[:, GROUP:]], axis=1)], axis=0)

    def sub_tile(r0):
        env = {}

        def ln0():
            h = _layer_norm(x_ref[r0:r0 + sub, :], ln0g_ref[...], ln0b_ref[...])
            env["h"] = h
            env["hb"] = h.astype(BF16)

        def proj(sec):
            cols = slice(sec * width, (sec + 1) * width)
            return _dot(env["hb"], win_ref[:, cols]) + bin_ref[:, cols]

        def hgrn_inputs():
            sig_f = jax.nn.sigmoid(proj(3))
            env["logf"] = jnp.log2(lb + (1.0 - lb) * sig_f)
            env["k"] = ((1.0 - lb) * (1.0 - sig_f)).astype(BF16)
            env["q"] = jax.nn.silu(proj(2)).astype(BF16)

        def chunk_prep(c):
            rows = slice(c * HG_CHUNK, (c + 1) * HG_CHUNK)
            qb, kb, logf = env["q"][rows], env["k"][rows], env["logf"][rows]
            b = _dot(cum_mat3, jnp.concatenate(_split3(logf), axis=0))
            b_last = b[HG_CHUNK - 1:HG_CHUNK, :]
            ch = {"rows": rows}
            ch["q_dec"] = qb * jnp.exp2(b).astype(BF16)
            ch["k_end"] = kb * jnp.exp2(b_last - b).astype(BF16)
            ch["decay"] = jnp.exp2(b_last)
            lvl = []
            for m, mask in level_masks:
                e = jnp.exp2(-jnp.abs(b - _row_ref(b, 2 * m, m - 1))).astype(BF16)
                lvl.append((qb * e, kb * e, mask))
            ch["lvl"] = lvl
            dq = b - (_row_ref(b, HG_DIAG, 0) - _row_ref(logf, HG_DIAG, 0))
            ch["q_d"] = qb * jnp.exp2(dq).astype(BF16)
            ch["k_d"] = kb * jnp.exp2(-dq).astype(BF16)
            env["chunk", c] = ch

        def proj_v():
            env["v"] = proj(4).astype(BF16)

        def proj_gate():
            env["gate"] = jax.nn.silu(proj(5)).astype(BF16)

        def head_pair(c, pr):
            ch = env["chunk", c]
            rows = ch["rows"]
            out_rows = slice(r0 + rows.start, r0 + rows.stop)
            vb = env["v"][rows]
            pc = slice(pr * pair, (pr + 1) * pair)
            scores = jnp.where(diag_mask, _dot_nt(ch["q_d"][:, pc], block_diag(ch["k_d"][:, pc])), 0.0)
            for ql, kl, mask in ch["lvl"]:
                scores = jnp.where(mask, _dot_nt(ql[:, pc], block_diag(kl[:, pc])), scores)
            st = state_ref[pr]
            o = (_dot(scores.astype(BF16), block_diag(vb[:, pc]))
                 + _dot_nt(ch["q_dec"][:, pc], st.astype(BF16)))
            upd = _dot_tn(vb[:, pc], ch["k_end"][:, pc])
            state_ref[pr] = st * ch["decay"][:, pc] + jnp.where(same_head, upd, 0.0)
            for hh in range(2):
                cols = slice(pr * pair + hh * GROUP, pr * pair + (hh + 1) * GROUP)
                oh = o[:, hh * GROUP:(hh + 1) * GROUP]
                oh = oh * lax.rsqrt(jnp.mean(oh * oh, -1, keepdims=True) + RMS_EPS) * hng[:, cols]
                yb_ref[out_rows, cols] = oh.astype(BF16) * env["gate"][rows, cols]

        def gmlp_u():
            env["u"] = jax.nn.gelu(proj(0)).astype(BF16)

        def gmlp_v():
            v = jax.nn.gelu(proj(1))
            env["vnb"] = _layer_norm(v, gng_ref[...], gnb_ref[...]).astype(BF16)

        def gmlp_mix(groups):
            chunk_causal = (t_i // STREAM_CHUNK) >= (s_i // STREAM_CHUNK)
            u, vnb = env["u"], env["vnb"]
            for g in groups:
                cols = slice(g * GROUP, (g + 1) * GROUP)
                wg = jnp.where(chunk_causal, gw_ref[g], 0.0).astype(BF16)
                for blk in range(0, sub // GM_BLOCK, 2):
                    r_a = slice(blk * GM_BLOCK, (blk + 1) * GM_BLOCK)
                    r_b = slice((blk + 1) * GM_BLOCK, (blk + 2) * GM_BLOCK)
                    mixed = _dot(wg, jnp.concatenate([vnb[r_a, cols], vnb[r_b, cols]], axis=1))
                    ya_ref[r0 + r_a.start:r0 + r_a.stop, cols] = (
                        u[r_a, cols] * (mixed[:, :GROUP] + gbias_ref[:, cols]).astype(BF16))
                    ya_ref[r0 + r_b.start:r0 + r_b.stop, cols] = (
                        u[r_b, cols] * (mixed[:, GROUP:] + gbias_ref[:, cols]).astype(BF16))

        def gate_a():
            env["ga"] = jax.nn.sigmoid(proj(6))

        def branch_a():
            env["ma"] = env["ga"] * _dot(ya_ref[r0:r0 + sub, :], wa_ref[...])

        def gate_b():
            env["gb"] = jax.nn.sigmoid(proj(7))

        def merge():
            merged = env["ma"] + env["gb"] * _dot(yb_ref[r0:r0 + sub, :], wb_ref[...])
            env["mix"] = _dot(merged.astype(BF16), wo_ref[...])

        def ln1():
            o_ref[r0:r0 + sub, :] = _layer_norm(alpha * env["h"] + env["mix"], ln1g_ref[...], ln1b_ref[...])

        body = [hgrn_inputs, gmlp_u, gmlp_v]
        early = [proj_v, proj_gate]
        for c in range(n_chunks):
            body.append(functools.partial(chunk_prep, c))
            if early:
                body.append(early.pop(0))
        body += early
        body += [functools.partial(gmlp_mix, range(0, n_groups // 2)),
                 functools.partial(gmlp_mix, range(n_groups // 2, n_groups))]
        fillers = [gate_a, branch_a, gate_b]
        units = [(c, pr) for c in range(n_chunks) for pr in range(n_heads // 2)]
        stride = max(1, len(units) // (len(fillers) + 1))
        for i, (c, pr) in enumerate(units):
            body.append(functools.partial(head_pair, c, pr))
            if i % stride == stride - 1 and fillers:
                body.append(fillers.pop(0))
        body += fillers
        return ln0, body, [merge, ln1]

    parts = [sub_tile(r0) for r0 in range(0, tm, sub)]
    parts[0][0]()
    for i, (_, body, tail) in enumerate(parts):
        for piece in body:
            piece()
        if i + 1 < len(parts):
            parts[i + 1][0]()
            nxt = parts[i + 1][1]
            tail[0]()
            nxt.pop(0)()
            tail[1]()
        else:
            for piece in tail:
                piece()


def _ffn_kernel(h_ref, p_ref, wg_ref, wu_ref, wd_ref, wpg_ref, bpg_ref, wple_ref, ln2g_ref, ln2b_ref,
                o_ref, *, alpha, sub):
    def residual_sum(rows):
        h1 = h_ref[rows, :]
        hb = h1.astype(BF16)
        hid = jax.nn.silu(_dot(hb, wg_ref[...])) * _dot(hb, wu_ref[...])
        ffn = _dot(hid.astype(BF16), wd_ref[...])
        ple = (jax.nn.sigmoid(_dot(hb, wpg_ref[...]) + bpg_ref[...])
               * _dot(p_ref[rows, :].astype(BF16), wple_ref[...]))
        return alpha * h1 + ffn + ple

    row_blocks = [slice(r, r + sub) for r in range(0, h_ref.shape[0], sub)]
    sums = [residual_sum(rows) for rows in row_blocks]
    for rows, pre in zip(row_blocks, sums):
        o_ref[rows, :] = _layer_norm(pre, ln2g_ref[...], ln2b_ref[...])


def _resident(arr):
    nd = arr.ndim
    return pl.BlockSpec(arr.shape, lambda *_: (0,) * nd, pipeline_mode=pl.Buffered(1))


def kernel(x, p, ln0_g, ln0_b, w_in, b_in, gm_norm_g, gm_norm_b, gm_w_s, gm_b_s, hg_lb_logits, hg_norm_g, w_a, w_b, w_o, ln1_g, ln1_b, w_ffn_gate, w_ffn_up, w_ffn_down, w_ple, w_ple_gate, b_ple_gate, ln2_g, ln2_b):
    bsz, seq, d = x.shape
    depth = w_in.shape[0]
    assert depth == 1, "single-layer block"
    assert seq % TM_MIX == 0 and (bsz * seq) % TM_FFN == 0
    assert TM_MIX % SUB_MIX == 0 and SUB_MIX % (2 * HG_CHUNK) == 0 and TM_FFN % SUB_FFN == 0
    assert w_in.shape[2] == 8 * d and d % (2 * GROUP) == 0
    alpha = (2.0 * depth) ** 0.25
    layer = 0
    row = lambda a: a.reshape(1, -1).astype(F32)

    gbias = jnp.repeat(gm_b_s[layer].T, GROUP, axis=1)
    mix_consts = [row(ln0_g), row(ln0_b), w_in[layer].astype(BF16), row(b_in[layer]),
                  row(gm_norm_g[layer]), row(gm_norm_b[layer]), gm_w_s[layer], gbias,
                  hg_lb_logits.astype(F32), row(hg_norm_g[layer]),
                  w_a[layer].astype(BF16), w_b[layer].astype(BF16), w_o[layer].astype(BF16),
                  row(ln1_g[layer]), row(ln1_b[layer])]
    h1 = pl.pallas_call(
        functools.partial(_mix_kernel, tm=TM_MIX, sub=SUB_MIX, width=d, layer=layer, alpha=alpha),
        name="token_mixing",
        grid=(bsz, seq // TM_MIX),
        in_specs=[pl.BlockSpec((None, TM_MIX, d), lambda b, s: (b, s, 0))]
                 + [_resident(a) for a in mix_consts],
        out_specs=pl.BlockSpec((None, TM_MIX, d), lambda b, s: (b, s, 0)),
        out_shape=jax.ShapeDtypeStruct((bsz, seq, d), F32),
        scratch_shapes=[pltpu.VMEM((d // (2 * GROUP), 2 * GROUP, 2 * GROUP), F32),
                        pltpu.VMEM((TM_MIX, d), BF16),
                        pltpu.VMEM((TM_MIX, d), BF16)],
        compiler_params=pltpu.CompilerParams(
            dimension_semantics=("arbitrary", "arbitrary"),
            vmem_limit_bytes=V7X_VMEM_LIMIT_BYTES),
    )(x, *mix_consts)

    n_tok = bsz * seq
    ple_dim = p.shape[-1]
    ffn_consts = [w_ffn_gate[layer].astype(BF16), w_ffn_up[layer].astype(BF16),
                  w_ffn_down[layer].astype(BF16), w_ple_gate[layer].astype(BF16),
                  row(b_ple_gate[layer]), w_ple[layer].astype(BF16),
                  row(ln2_g[layer]), row(ln2_b[layer])]
    out = pl.pallas_call(
        functools.partial(_ffn_kernel, alpha=alpha, sub=SUB_FFN),
        name="channel_mixing",
        grid=(n_tok // TM_FFN,),
        in_specs=[pl.BlockSpec((TM_FFN, d), lambda i: (i, 0)),
                  pl.BlockSpec((TM_FFN, ple_dim), lambda i: (i, 0))]
                 + [_resident(a) for a in ffn_consts],
        out_specs=pl.BlockSpec((TM_FFN, d), lambda i: (i, 0)),
        out_shape=jax.ShapeDtypeStruct((n_tok, d), F32),
        compiler_params=pltpu.CompilerParams(
            dimension_semantics=("arbitrary",),
            vmem_limit_bytes=V7X_VMEM_LIMIT_BYTES),
    )(h1.reshape(n_tok, d), p[layer].reshape(n_tok, ple_dim), *ffn_consts)
    return out.reshape(bsz, seq, d)
```

```python
import functools

import jax
import jax.numpy as jnp
from jax import lax
from jax.experimental import pallas as pl
from jax.experimental.pallas import tpu as pltpu

F32 = jnp.float32
BF16 = jnp.bfloat16

LN_EPS = 1e-5
RMS_EPS = 1e-6
STREAM_CHUNK = 64
GM_BLOCK = 128
GROUP = 128
HG_CHUNK = 128
HG_DIAG = 16
V7X_VMEM_LIMIT_BYTES = 56 * 1024 * 1024

TM_MIX = 512
SUB_MIX = 256
TM_FFN = 1024
SUB_FFN = 256


def _layer_norm(x, g, b):
    mu = jnp.mean(x, -1, keepdims=True)
    xc = x - mu
    var = jnp.mean(xc * xc, -1, keepdims=True)
    return xc * lax.rsqrt(var + LN_EPS) * g + b


def _dot(a, b):
    return jnp.dot(a, b, preferred_element_type=F32)


def _dot_nt(a, b):
    return lax.dot_general(a, b, (((1,), (1,)), ((), ())), preferred_element_type=F32)


def _dot_tn(a, b):
    return lax.dot_general(a, b, (((0,), (0,)), ((), ())), preferred_element_type=F32)


def _split3(x):
    hi = x.astype(BF16)
    r1 = x - hi.astype(F32)
    mid = r1.astype(BF16)
    lo = (r1 - mid.astype(F32)).astype(BF16)
    return hi, mid, lo


def _row_ref(b, period, row):
    parts = []
    for j in range(HG_CHUNK // period):
        r = j * period + row
        parts.append(jnp.broadcast_to(b[r:r + 1, :], (period, b.shape[1])))
    return parts[0] if len(parts) == 1 else jnp.concatenate(parts, axis=0)


def _mix_kernel(x_ref, ln0g_ref, ln0b_ref, win_ref, bin_ref, gng_ref, gnb_ref, gw_ref, gbias_ref,
                lbl_ref, hng_ref, wa_ref, wb_ref, wo_ref, ln1g_ref, ln1b_ref,
                o_ref, state_ref, ya_ref, yb_ref, *, tm, sub, width, layer, alpha):
    n_heads = width // GROUP
    n_groups = width // GROUP
    n_chunks = sub // HG_CHUNK
    pair = 2 * GROUP

    @pl.when(pl.program_id(1) == 0)
    def _():
        state_ref[...] = jnp.zeros_like(state_ref)

    t_i = lax.broadcasted_iota(jnp.int32, (GM_BLOCK, GM_BLOCK), 0)
    s_i = lax.broadcasted_iota(jnp.int32, (GM_BLOCK, GM_BLOCK), 1)
    logits = lbl_ref[...]
    e_l = jnp.exp(logits - jnp.max(logits, axis=0, keepdims=True))
    lb = jnp.sum(e_l[:layer + 1], axis=0, keepdims=True) / jnp.sum(e_l, axis=0, keepdims=True)
    hng = hng_ref[...]
    cum_mat = jnp.where(t_i >= s_i, 1.0, 0.0).astype(BF16)
    cum_mat3 = jnp.concatenate([cum_mat, cum_mat, cum_mat], axis=1)
    t_p = lax.broadcasted_iota(jnp.int32, (HG_CHUNK, pair), 0)
    s_p = lax.broadcasted_iota(jnp.int32, (HG_CHUNK, pair), 1) % GROUP
    level_masks = []
    m = HG_CHUNK // 2
    while m >= HG_DIAG:
        level_masks.append((m, (t_p // (2 * m) == s_p // (2 * m)) & (t_p % (2 * m) >= m)
                            & (s_p % (2 * m) < m)))
        m //= 2
    diag_mask = (t_p // HG_DIAG == s_p // HG_DIAG) & (t_p >= s_p)
    same_head = (lax.broadcasted_iota(jnp.int32, (pair, pair), 0) // GROUP
                 == lax.broadcasted_iota(jnp.int32, (pair, pair), 1) // GROUP)
    zero_blk = jnp.zeros((HG_CHUNK, GROUP), BF16)

    def block_diag(xp):
        return jnp.concatenate([jnp.concatenate([xp[:, :GROUP], zero_blk], axis=1),
                                jnp.concatenate([zero_blk, xp[:, GROUP:]], axis=1)], axis=0)

    def sub_tile(r0):
        env = {}

        def ln0():
            h = _layer_norm(x_ref[r0:r0 + sub, :], ln0g_ref[...], ln0b_ref[...])
            env["h"] = h
            env["hb"] = h.astype(BF16)

        def proj(sec):
            cols = slice(sec * width, (sec + 1) * width)
            return _dot(env["hb"], win_ref[:, cols]) + bin_ref[:, cols]

        def hgrn_inputs():
            sig_f = jax.nn.sigmoid(proj(3))
            env["logf"] = jnp.log2(lb + (1.0 - lb) * sig_f)
            env["k"] = (1.0 - lb) * (1.0 - sig_f)
            env["q"] = jax.nn.silu(proj(2))

        def chunk_prep(c):
            rows = slice(c * HG_CHUNK, (c + 1) * HG_CHUNK)
            q, k, logf = env["q"][rows], env["k"][rows], env["logf"][rows]
            b = _dot(cum_mat3, jnp.concatenate(_split3(logf), axis=0))
            b_last = b[HG_CHUNK - 1:HG_CHUNK, :]
            ch = {"rows": rows}
            qb, kb = q.astype(BF16), k.astype(BF16)
            ch["q_dec"] = qb * jnp.exp2(b).astype(BF16)
            ch["k_end"] = kb * jnp.exp2(b_last - b).astype(BF16)
            ch["decay"] = jnp.exp2(b_last)
            lvl = []
            for m, mask in level_masks:
                e = jnp.exp2(-jnp.abs(b - _row_ref(b, 2 * m, m - 1))).astype(BF16)
                lvl.append((qb * e, kb * e, mask))
            ch["lvl"] = lvl
            dq = b - (_row_ref(b, HG_DIAG, 0) - _row_ref(logf, HG_DIAG, 0))
            ch["q_d"] = qb * jnp.exp2(dq).astype(BF16)
            ch["k_d"] = kb * jnp.exp2(-dq).astype(BF16)
            env["chunk", c] = ch

        def proj_v():
            env["v"] = proj(4).astype(BF16)

        def proj_gate():
            env["gate"] = jax.nn.silu(proj(5)).astype(BF16)

        def head_pair(c, pr):
            ch = env["chunk", c]
            rows = ch["rows"]
            out_rows = slice(r0 + rows.start, r0 + rows.stop)
            vb = env["v"][rows]
            pc = slice(pr * pair, (pr + 1) * pair)
            scores = jnp.where(diag_mask, _dot_nt(ch["q_d"][:, pc], block_diag(ch["k_d"][:, pc])), 0.0)
            for ql, kl, mask in ch["lvl"]:
                scores = jnp.where(mask, _dot_nt(ql[:, pc], block_diag(kl[:, pc])), scores)
            st = state_ref[pr]
            o = (_dot(scores.astype(BF16), block_diag(vb[:, pc]))
                 + _dot_nt(ch["q_dec"][:, pc], st.astype(BF16)))
            upd = _dot_tn(vb[:, pc], ch["k_end"][:, pc])
            state_ref[pr] = st * ch["decay"][:, pc] + jnp.where(same_head, upd, 0.0)
            for hh in range(2):
                cols = slice(pr * pair + hh * GROUP, pr * pair + (hh + 1) * GROUP)
                oh = o[:, hh * GROUP:(hh + 1) * GROUP]
                oh = oh * lax.rsqrt(jnp.mean(oh * oh, -1, keepdims=True) + RMS_EPS) * hng[:, cols]
                yb_ref[out_rows, cols] = oh.astype(BF16) * env["gate"][rows, cols]

        def gmlp_u():
            env["u"] = jax.nn.gelu(proj(0)).astype(BF16)

        def gmlp_v():
            v = jax.nn.gelu(proj(1))
            env["vnb"] = _layer_norm(v, gng_ref[...], gnb_ref[...]).astype(BF16)

        def gmlp_mix(groups):
            chunk_causal = (t_i // STREAM_CHUNK) >= (s_i // STREAM_CHUNK)
            u, vnb = env["u"], env["vnb"]
            for g in groups:
                cols = slice(g * GROUP, (g + 1) * GROUP)
                wg = jnp.where(chunk_causal, gw_ref[g], 0.0).astype(BF16)
                for blk in range(0, sub // GM_BLOCK, 2):
                    r_a = slice(blk * GM_BLOCK, (blk + 1) * GM_BLOCK)
                    r_b = slice((blk + 1) * GM_BLOCK, (blk + 2) * GM_BLOCK)
                    mixed = _dot(wg, jnp.concatenate([vnb[r_a, cols], vnb[r_b, cols]], axis=1))
                    ya_ref[r0 + r_a.start:r0 + r_a.stop, cols] = (
                        u[r_a, cols] * (mixed[:, :GROUP] + gbias_ref[:, cols]).astype(BF16))
                    ya_ref[r0 + r_b.start:r0 + r_b.stop, cols] = (
                        u[r_b, cols] * (mixed[:, GROUP:] + gbias_ref[:, cols]).astype(BF16))

        def gate_a():
            env["ga"] = jax.nn.sigmoid(proj(6))

        def branch_a():
            env["ma"] = env["ga"] * _dot(ya_ref[r0:r0 + sub, :], wa_ref[...])

        def gate_b():
            env["gb"] = jax.nn.sigmoid(proj(7))

        def merge():
            merged = env["ma"] + env["gb"] * _dot(yb_ref[r0:r0 + sub, :], wb_ref[...])
            env["mix"] = _dot(merged.astype(BF16), wo_ref[...])

        def ln1():
            o_ref[r0:r0 + sub, :] = _layer_norm(alpha * env["h"] + env["mix"], ln1g_ref[...], ln1b_ref[...])

        body = [hgrn_inputs, gmlp_u, gmlp_v]
        early = [proj_v, proj_gate]
        for c in range(n_chunks):
            body.append(functools.partial(chunk_prep, c))
            if early:
                body.append(early.pop(0))
        body += early
        body += [functools.partial(gmlp_mix, range(0, n_groups // 2)),
                 functools.partial(gmlp_mix, range(n_groups // 2, n_groups))]
        fillers = [gate_a, branch_a, gate_b]
        units = [(c, pr) for c in range(n_chunks) for pr in range(n_heads // 2)]
        stride = max(1, len(units) // (len(fillers) + 1))
        for i, (c, pr) in enumerate(units):
            body.append(functools.partial(head_pair, c, pr))
            if i % stride == stride - 1 and fillers:
                body.append(fillers.pop(0))
        body += fillers
        return ln0, body, [merge, ln1]

    parts = [sub_tile(r0) for r0 in range(0, tm, sub)]
    parts[0][0]()
    for i, (_, body, tail) in enumerate(parts):
        for piece in body:
            piece()
        if i + 1 < len(parts):
            parts[i + 1][0]()
            nxt = parts[i + 1][1]
            tail[0]()
            nxt.pop(0)()
            tail[1]()
        else:
            for piece in tail:
                piece()


def _ffn_kernel(h_ref, p_ref, wg_ref, wu_ref, wd_ref, wpg_ref, bpg_ref, wple_ref, ln2g_ref, ln2b_ref,
                o_ref, *, alpha, sub):
    def residual_sum(rows):
        h1 = h_ref[rows, :]
        hb = h1.astype(BF16)
        hid = jax.nn.silu(_dot(hb, wg_ref[...])) * _dot(hb, wu_ref[...])
        ffn = _dot(hid.astype(BF16), wd_ref[...])
        ple = (jax.nn.sigmoid(_dot(hb, wpg_ref[...]) + bpg_ref[...])
               * _dot(p_ref[rows, :].astype(BF16), wple_ref[...]))
        return alpha * h1 + ffn + ple

    row_blocks = [slice(r, r + sub) for r in range(0, h_ref.shape[0], sub)]
    sums = [residual_sum(rows) for rows in row_blocks]
    for rows, pre in zip(row_blocks, sums):
        o_ref[rows, :] = _layer_norm(pre, ln2g_ref[...], ln2b_ref[...])


def _resident(arr):
    nd = arr.ndim
    return pl.BlockSpec(arr.shape, lambda *_: (0,) * nd, pipeline_mode=pl.Buffered(1))


def kernel(x, p, ln0_g, ln0_b, w_in, b_in, gm_norm_g, gm_norm_b, gm_w_s, gm_b_s, hg_lb_logits, hg_norm_g, w_a, w_b, w_o, ln1_g, ln1_b, w_ffn_gate, w_ffn_up, w_ffn_down, w_ple, w_ple_gate, b_ple_gate, ln2_g, ln2_b):
    bsz, seq, d = x.shape
    depth = w_in.shape[0]
    assert depth == 1, "single-layer block"
    assert seq % TM_MIX == 0 and (bsz * seq) % TM_FFN == 0
    assert TM_MIX % SUB_MIX == 0 and SUB_MIX % (2 * HG_CHUNK) == 0 and TM_FFN % SUB_FFN == 0
    assert w_in.shape[2] == 8 * d and d % (2 * GROUP) == 0
    alpha = (2.0 * depth) ** 0.25
    layer = 0
    row = lambda a: a.reshape(1, -1).astype(F32)

    gbias = jnp.repeat(gm_b_s[layer].T, GROUP, axis=1)
    mix_consts = [row(ln0_g), row(ln0_b), w_in[layer].astype(BF16), row(b_in[layer]),
                  row(gm_norm_g[layer]), row(gm_norm_b[layer]), gm_w_s[layer], gbias,
                  hg_lb_logits.astype(F32), row(hg_norm_g[layer]),
                  w_a[layer].astype(BF16), w_b[layer].astype(BF16), w_o[layer].astype(BF16),
                  row(ln1_g[layer]), row(ln1_b[layer])]
    h1 = pl.pallas_call(
        functools.partial(_mix_kernel, tm=TM_MIX, sub=SUB_MIX, width=d, layer=layer, alpha=alpha),
        name="token_mixing",
        grid=(bsz, seq // TM_MIX),
        in_specs=[pl.BlockSpec((None, TM_MIX, d), lambda b, s: (b, s, 0))]
                 + [_resident(a) for a in mix_consts],
        out_specs=pl.BlockSpec((None, TM_MIX, d), lambda b, s: (b, s, 0)),
        out_shape=jax.ShapeDtypeStruct((bsz, seq, d), F32),
        scratch_shapes=[pltpu.VMEM((d // (2 * GROUP), 2 * GROUP, 2 * GROUP), F32),
                        pltpu.VMEM((TM_MIX, d), BF16),
                        pltpu.VMEM((TM_MIX, d), BF16)],
        compiler_params=pltpu.CompilerParams(
            dimension_semantics=("arbitrary", "arbitrary"),
            vmem_limit_bytes=V7X_VMEM_LIMIT_BYTES),
    )(x, *mix_consts)

    n_tok = bsz * seq
    ple_dim = p.shape[-1]
    ffn_consts = [w_ffn_gate[layer].astype(BF16), w_ffn_up[layer].astype(BF16),
                  w_ffn_down[layer].astype(BF16), w_ple_gate[layer].astype(BF16),
                  row(b_ple_gate[layer]), w_ple[layer].astype(BF16),
                  row(ln2_g[layer]), row(ln2_b[layer])]
    out = pl.pallas_call(
        functools.partial(_ffn_kernel, alpha=alpha, sub=SUB_FFN),
        name="channel_mixing",
        grid=(n_tok // TM_FFN,),
        in_specs=[pl.BlockSpec((TM_FFN, d), lambda i: (i, 0)),
                  pl.BlockSpec((TM_FFN, ple_dim), lambda i: (i, 0))]
                 + [_resident(a) for a in ffn_consts],
        out_specs=pl.BlockSpec((TM_FFN, d), lambda i: (i, 0)),
        out_shape=jax.ShapeDtypeStruct((n_tok, d), F32),
        compiler_params=pltpu.CompilerParams(
            dimension_semantics=("arbitrary",),
            vmem_limit_bytes=V7X_VMEM_LIMIT_BYTES),
    )(h1.reshape(n_tok, d), p[layer].reshape(n_tok, ple_dim), *ffn_consts)
    return out.reshape(bsz, seq, d)
```

```python
import functools

import jax
import jax.numpy as jnp
from jax import lax
from jax.experimental import pallas as pl
from jax.experimental.pallas import tpu as pltpu

F32 = jnp.float32
BF16 = jnp.bfloat16

LN_EPS = 1e-5
RMS_EPS = 1e-6
STREAM_CHUNK = 64
GM_BLOCK = 128
GROUP = 128
HG_CHUNK = 128
HG_DIAG = 16
V7X_VMEM_LIMIT_BYTES = 56 * 1024 * 1024

TM_MIX = 512
SUB_MIX = 256
TM_FFN = 1024
SUB_FFN = 256


def _layer_norm(x, g, b):
    mu = jnp.mean(x, -1, keepdims=True)
    xc = x - mu
    var = jnp.mean(xc * xc, -1, keepdims=True)
    return xc * lax.rsqrt(var + LN_EPS) * g + b


def _dot(a, b):
    return jnp.dot(a, b, preferred_element_type=F32)


def _dot_nt(a, b):
    return lax.dot_general(a, b, (((1,), (1,)), ((), ())), preferred_element_type=F32)


def _dot_tn(a, b):
    return lax.dot_general(a, b, (((0,), (0,)), ((), ())), preferred_element_type=F32)


def _split2(x):
    hi = x.astype(BF16)
    mid = (x - hi.astype(F32)).astype(BF16)
    return hi, mid


def _row_ref(b, period, row):
    parts = []
    for j in range(HG_CHUNK // period):
        r = j * period + row
        parts.append(jnp.broadcast_to(b[r:r + 1, :], (period, b.shape[1])))
    return parts[0] if len(parts) == 1 else jnp.concatenate(parts, axis=0)


def _mix_kernel(x_ref, ln0g_ref, ln0b_ref, win_ref, bin_ref, gng_ref, gnb_ref, gw_ref, gbias_ref,
                lbl_ref, hng_ref, wa_ref, wb_ref, wo_ref, ln1g_ref, ln1b_ref,
                o_ref, state_ref, ya_ref, yb_ref, *, tm, sub, width, layer, alpha):
    n_heads = width // GROUP
    n_groups = width // GROUP
    n_chunks = sub // HG_CHUNK
    pair = 2 * GROUP

    @pl.when(pl.program_id(1) == 0)
    def _():
        state_ref[...] = jnp.zeros_like(state_ref)

    t_i = lax.broadcasted_iota(jnp.int32, (GM_BLOCK, GM_BLOCK), 0)
    s_i = lax.broadcasted_iota(jnp.int32, (GM_BLOCK, GM_BLOCK), 1)
    logits = lbl_ref[...]
    e_l = jnp.exp(logits - jnp.max(logits, axis=0, keepdims=True))
    lb = jnp.sum(e_l[:layer + 1], axis=0, keepdims=True) / jnp.sum(e_l, axis=0, keepdims=True)
    hng = hng_ref[...]
    cum_mat = jnp.where(t_i >= s_i, 1.0, 0.0).astype(BF16)
    cum_mat2 = jnp.concatenate([cum_mat, cum_mat], axis=1)
    t_p = lax.broadcasted_iota(jnp.int32, (HG_CHUNK, pair), 0)
    s_p = lax.broadcasted_iota(jnp.int32, (HG_CHUNK, pair), 1) % GROUP
    level_masks = []
    m = HG_CHUNK // 2
    while m >= HG_DIAG:
        level_masks.append((m, (t_p // (2 * m) == s_p // (2 * m)) & (t_p % (2 * m) >= m)
                            & (s_p % (2 * m) < m)))
        m //= 2
    diag_mask = (t_p // HG_DIAG == s_p // HG_DIAG) & (t_p >= s_p)
    same_head = (lax.broadcasted_iota(jnp.int32, (pair, pair), 0) // GROUP
                 == lax.broadcasted_iota(jnp.int32, (pair, pair), 1) // GROUP)
    zero_blk = jnp.zeros((HG_CHUNK, GROUP), BF16)

    def block_diag(xp):
        return jnp.concatenate([jnp.concatenate([xp[:, :GROUP], zero_blk], axis=1),
                                jnp.concatenate([zero_blk, xp[:, GROUP:]], axis=1)], axis=0)

    def sub_tile(r0):
        env = {}

        def ln0():
            h = _layer_norm(x_ref[r0:r0 + sub, :], ln0g_ref[...], ln0b_ref[...])
            env["h"] = h
            env["hb"] = h.astype(BF16)

        def proj(sec):
            cols = slice(sec * width, (sec + 1) * width)
            return _dot(env["hb"], win_ref[:, cols]) + bin_ref[:, cols]

        def hgrn_inputs():
            sig_f = jax.nn.sigmoid(proj(3))
            env["logf"] = jnp.log2(lb + (1.0 - lb) * sig_f)
            env["k"] = (1.0 - lb) * (1.0 - sig_f)
            env["q"] = jax.nn.silu(proj(2))

        def chunk_prep(c):
            rows = slice(c * HG_CHUNK, (c + 1) * HG_CHUNK)
            q, k, logf = env["q"][rows], env["k"][rows], env["logf"][rows]
            b = _dot(cum_mat2, jnp.concatenate(_split2(logf), axis=0))
            b_last = b[HG_CHUNK - 1:HG_CHUNK, :]
            ch = {"rows": rows}
            qb, kb = q.astype(BF16), k.astype(BF16)
            ch["q_dec"] = qb * jnp.exp2(b).astype(BF16)
            ch["k_end"] = kb * jnp.exp2(b_last - b).astype(BF16)
            ch["decay"] = jnp.exp2(b_last)
            lvl = []
            for m, mask in level_masks:
                e = jnp.exp2(-jnp.abs(b - _row_ref(b, 2 * m, m - 1))).astype(BF16)
                lvl.append((qb * e, kb * e, mask))
            ch["lvl"] = lvl
            dq = b - (_row_ref(b, HG_DIAG, 0) - _row_ref(logf, HG_DIAG, 0))
            ch["q_d"] = qb * jnp.exp2(dq).astype(BF16)
            ch["k_d"] = kb * jnp.exp2(-dq).astype(BF16)
            env["chunk", c] = ch

        def proj_v():
            env["v"] = proj(4).astype(BF16)

        def proj_gate():
            env["gate"] = jax.nn.silu(proj(5)).astype(BF16)

        def head_pair(c, pr):
            ch = env["chunk", c]
            rows = ch["rows"]
            out_rows = slice(r0 + rows.start, r0 + rows.stop)
            vb = env["v"][rows]
            pc = slice(pr * pair, (pr + 1) * pair)
            scores = jnp.where(diag_mask, _dot_nt(ch["q_d"][:, pc], block_diag(ch["k_d"][:, pc])), 0.0)
            for ql, kl, mask in ch["lvl"]:
                scores = jnp.where(mask, _dot_nt(ql[:, pc], block_diag(kl[:, pc])), scores)
            st = state_ref[pr]
            o = (_dot(scores.astype(BF16), block_diag(vb[:, pc]))
                 + _dot_nt(ch["q_dec"][:, pc], st.astype(BF16)))
            upd = _dot_tn(vb[:, pc], ch["k_end"][:, pc])
            state_ref[pr] = st * ch["decay"][:, pc] + jnp.where(same_head, upd, 0.0)
            for hh in range(2):
                cols = slice(pr * pair + hh * GROUP, pr * pair + (hh + 1) * GROUP)
                oh = o[:, hh * GROUP:(hh + 1) * GROUP]
                oh = oh * lax.rsqrt(jnp.mean(oh * oh, -1, keepdims=True) + RMS_EPS) * hng[:, cols]
                yb_ref[out_rows, cols] = oh.astype(BF16) * env["gate"][rows, cols]

        def gmlp_u():
            env["u"] = jax.nn.gelu(proj(0)).astype(BF16)

        def gmlp_v():
            v = jax.nn.gelu(proj(1))
            env["vnb"] = _layer_norm(v, gng_ref[...], gnb_ref[...]).astype(BF16)

        def gmlp_mix(groups):
            chunk_causal = (t_i // STREAM_CHUNK) >= (s_i // STREAM_CHUNK)
            u, vnb = env["u"], env["vnb"]
            for g in groups:
                cols = slice(g * GROUP, (g + 1) * GROUP)
                wg = jnp.where(chunk_causal, gw_ref[g], 0.0).astype(BF16)
                for blk in range(0, sub // GM_BLOCK, 2):
                    r_a = slice(blk * GM_BLOCK, (blk + 1) * GM_BLOCK)
                    r_b = slice((blk + 1) * GM_BLOCK, (blk + 2) * GM_BLOCK)
                    mixed = _dot(wg, jnp.concatenate([vnb[r_a, cols], vnb[r_b, cols]], axis=1))
                    ya_ref[r0 + r_a.start:r0 + r_a.stop, cols] = (
                        u[r_a, cols] * (mixed[:, :GROUP] + gbias_ref[:, cols]).astype(BF16))
                    ya_ref[r0 + r_b.start:r0 + r_b.stop, cols] = (
                        u[r_b, cols] * (mixed[:, GROUP:] + gbias_ref[:, cols]).astype(BF16))

        def gate_a():
            env["ga"] = jax.nn.sigmoid(proj(6))

        def branch_a():
            env["ma"] = env["ga"] * _dot(ya_ref[r0:r0 + sub, :], wa_ref[...])

        def gate_b():
            env["gb"] = jax.nn.sigmoid(proj(7))

        def merge():
            merged = env["ma"] + env["gb"] * _dot(yb_ref[r0:r0 + sub, :], wb_ref[...])
            env["mix"] = _dot(merged.astype(BF16), wo_ref[...])

        def ln1():
            o_ref[r0:r0 + sub, :] = _layer_norm(alpha * env["h"] + env["mix"], ln1g_ref[...], ln1b_ref[...])

        body = [hgrn_inputs, gmlp_u, gmlp_v]
        early = [proj_v, proj_gate]
        for c in range(n_chunks):
            body.append(functools.partial(chunk_prep, c))
            if early:
                body.append(early.pop(0))
        body += early
        body += [functools.partial(gmlp_mix, range(0, n_groups // 2)),
                 functools.partial(gmlp_mix, range(n_groups // 2, n_groups))]
        fillers = [gate_a, branch_a, gate_b]
        units = [(c, pr) for c in range(n_chunks) for pr in range(n_heads // 2)]
        stride = max(1, len(units) // (len(fillers) + 1))
        for i, (c, pr) in enumerate(units):
            body.append(functools.partial(head_pair, c, pr))
            if i % stride == stride - 1 and fillers:
                body.append(fillers.pop(0))
        body += fillers
        return ln0, body, [merge, ln1]

    parts = [sub_tile(r0) for r0 in range(0, tm, sub)]
    parts[0][0]()
    for i, (_, body, tail) in enumerate(parts):
        for piece in body:
            piece()
        if i + 1 < len(parts):
            parts[i + 1][0]()
            nxt = parts[i + 1][1]
            tail[0]()
            nxt.pop(0)()
            tail[1]()
        else:
            for piece in tail:
                piece()


def _ffn_kernel(h_ref, p_ref, wg_ref, wu_ref, wd_ref, wpg_ref, bpg_ref, wple_ref, ln2g_ref, ln2b_ref,
                o_ref, *, alpha, sub):
    def residual_sum(rows):
        h1 = h_ref[rows, :]
        hb = h1.astype(BF16)
        hid = jax.nn.silu(_dot(hb, wg_ref[...])) * _dot(hb, wu_ref[...])
        ffn = _dot(hid.astype(BF16), wd_ref[...])
        ple = (jax.nn.sigmoid(_dot(hb, wpg_ref[...]) + bpg_ref[...])
               * _dot(p_ref[rows, :].astype(BF16), wple_ref[...]))
        return alpha * h1 + ffn + ple

    row_blocks = [slice(r, r + sub) for r in range(0, h_ref.shape[0], sub)]
    sums = [residual_sum(rows) for rows in row_blocks]
    for rows, pre in zip(row_blocks, sums):
        o_ref[rows, :] = _layer_norm(pre, ln2g_ref[...], ln2b_ref[...])


def _resident(arr):
    nd = arr.ndim
    return pl.BlockSpec(arr.shape, lambda *_: (0,) * nd, pipeline_mode=pl.Buffered(1))


def kernel(x, p, ln0_g, ln0_b, w_in, b_in, gm_norm_g, gm_norm_b, gm_w_s, gm_b_s, hg_lb_logits, hg_norm_g, w_a, w_b, w_o, ln1_g, ln1_b, w_ffn_gate, w_ffn_up, w_ffn_down, w_ple, w_ple_gate, b_ple_gate, ln2_g, ln2_b):
    bsz, seq, d = x.shape
    depth = w_in.shape[0]
    assert depth == 1, "single-layer block"
    assert seq % TM_MIX == 0 and (bsz * seq) % TM_FFN == 0
    assert TM_MIX % SUB_MIX == 0 and SUB_MIX % (2 * HG_CHUNK) == 0 and TM_FFN % SUB_FFN == 0
    assert w_in.shape[2] == 8 * d and d % (2 * GROUP) == 0
    alpha = (2.0 * depth) ** 0.25
    layer = 0
    row = lambda a: a.reshape(1, -1).astype(F32)

    gbias = jnp.repeat(gm_b_s[layer].T, GROUP, axis=1)
    mix_consts = [row(ln0_g), row(ln0_b), w_in[layer].astype(BF16), row(b_in[layer]),
                  row(gm_norm_g[layer]), row(gm_norm_b[layer]), gm_w_s[layer], gbias,
                  hg_lb_logits.astype(F32), row(hg_norm_g[layer]),
                  w_a[layer].astype(BF16), w_b[layer].astype(BF16), w_o[layer].astype(BF16),
                  row(ln1_g[layer]), row(ln1_b[layer])]
    h1 = pl.pallas_call(
        functools.partial(_mix_kernel, tm=TM_MIX, sub=SUB_MIX, width=d, layer=layer, alpha=alpha),
        name="token_mixing",
        grid=(bsz, seq // TM_MIX),
        in_specs=[pl.BlockSpec((None, TM_MIX, d), lambda b, s: (b, s, 0))]
                 + [_resident(a) for a in mix_consts],
        out_specs=pl.BlockSpec((None, TM_MIX, d), lambda b, s: (b, s, 0)),
        out_shape=jax.ShapeDtypeStruct((bsz, seq, d), F32),
        scratch_shapes=[pltpu.VMEM((d // (2 * GROUP), 2 * GROUP, 2 * GROUP), F32),
                        pltpu.VMEM((TM_MIX, d), BF16),
                        pltpu.VMEM((TM_MIX, d), BF16)],
        compiler_params=pltpu.CompilerParams(
            dimension_semantics=("arbitrary", "arbitrary"),
            vmem_limit_bytes=V7X_VMEM_LIMIT_BYTES),
    )(x, *mix_consts)

    n_tok = bsz * seq
    ple_dim = p.shape[-1]
    ffn_consts = [w_ffn_gate[layer].astype(BF16), w_ffn_up[layer].astype(BF16),
                  w_ffn_down[layer].astype(BF16), w_ple_gate[layer].astype(BF16),
                  row(b_ple_gate[layer]), w_ple[layer].astype(BF16),
                  row(ln2_g[layer]), row(ln2_b[layer])]
    out = pl.pallas_call(
        functools.partial(_ffn_kernel, alpha=alpha, sub=SUB_FFN),
        name="channel_mixing",
        grid=(n_tok // TM_FFN,),
        in_specs=[pl.BlockSpec((TM_FFN, d), lambda i: (i, 0)),
                  pl.BlockSpec((TM_FFN, ple_dim), lambda i: (i, 0))]
                 + [_resident(a) for a in ffn_consts],
        out_specs=pl.BlockSpec((TM_FFN, d), lambda i: (i, 0)),
        out_shape=jax.ShapeDtypeStruct((n_tok, d), F32),
        compiler_params=pltpu.CompilerParams(
            dimension_semantics=("arbitrary",),
            vmem_limit_bytes=V7X_VMEM_LIMIT_BYTES),
    )(h1.reshape(n_tok, d), p[layer].reshape(n_tok, ple_dim), *ffn_consts)
    return out.reshape(bsz, seq, d)
```
